```python
import math
import jax, jax.numpy as jnp
from jax import lax
import numpy as np

D_MODEL = 4096
BATCH = 4
SEQ = 4096
DEPTH = 1

D_MIX = D_MODEL
D_SSM = D_MIX // 2
D_ATTN = D_MIX - D_SSM
SSM_HEAD_DIM = 64
SSM_HEADS = D_SSM // SSM_HEAD_DIM
SSM_GROUPS = 4
SSM_HPG = SSM_HEADS // SSM_GROUPS
SSM_STATE = 128
CONV_WIDTH = 4
CONV_CH = D_SSM + 2 * SSM_GROUPS * SSM_STATE
CHUNK = 128
ATTN_HEAD_DIM = 64
ATTN_VDIM = 2 * ATTN_HEAD_DIM
ATTN_HEADS = D_ATTN // ATTN_VDIM
Q_BLOCK = 128
NORM_EPS = 1e-6
IN_COLS = D_MIX + CONV_CH + SSM_HEADS + 3 * D_ATTN

kernel_name = "hybrid_ssd_diffattn_parallel_heads"


def rms_norm(x, w, eps=NORM_EPS):
    xf = x.astype(jnp.float32)
    y = xf * lax.rsqrt(jnp.mean(xf * xf, axis=-1, keepdims=True) + eps)
    return (y * w.astype(jnp.float32)).astype(x.dtype)


def segsum_exp(a):
    t = a.shape[-1]
    cs = jnp.cumsum(a, axis=-1)
    diff = cs[..., :, None] - cs[..., None, :]
    mask = jnp.tril(jnp.ones((t, t), dtype=bool))
    return jnp.exp(jnp.where(mask, diff, -jnp.inf))


def causal_depthwise_conv(u, w, b):
    y = lax.conv_general_dilated(
        u, w[:, None, :].astype(u.dtype), window_strides=(1,),
        padding=[(CONV_WIDTH - 1, 0)], dimension_numbers=("NWC", "WIO", "NWC"),
        feature_group_count=u.shape[-1])
    return y + b.astype(u.dtype)


def ssd_chunked(xh, dt, a_head, bg, cg):
    b, s = xh.shape[:2]
    nc = s // CHUNK
    X = (xh * dt[..., None]).reshape(b, nc, CHUNK, SSM_GROUPS, SSM_HPG, SSM_HEAD_DIM)
    a = jnp.moveaxis((dt * a_head).reshape(b, nc, CHUNK, SSM_GROUPS, SSM_HPG), 2, -1)
    Bc = bg.reshape(b, nc, CHUNK, SSM_GROUPS, SSM_STATE)
    Cc = cg.reshape(b, nc, CHUNK, SSM_GROUPS, SSM_STATE)
    a_cs = jnp.cumsum(a, axis=-1)
    L = segsum_exp(a)
    CB = jnp.einsum("bclgn,bcsgn->bcgls", Cc, Bc)
    y_diag = jnp.einsum("bcgls,bcgrls,bcsgrp->bclgrp", CB, L, X)
    decay_states = jnp.exp(a_cs[..., -1:] - a_cs)
    states = jnp.einsum("bclgn,bcgrl,bclgrp->bcgrpn", Bc, decay_states, X)
    states = jnp.concatenate([jnp.zeros_like(states[:, :1]), states], axis=1)
    chunk_a = jnp.moveaxis(a_cs[..., -1], 1, -1)
    chunk_a = jnp.pad(chunk_a, ((0, 0), (0, 0), (0, 0), (1, 0)))
    decay_chunk = segsum_exp(chunk_a)
    states_in = jnp.einsum("bgrzc,bcgrpn->bzgrpn", decay_chunk, states)[:, :-1]
    y_off = jnp.einsum("bclgn,bcgrpn,bcgrl->bclgrp", Cc, states_in, jnp.exp(a_cs))
    return (y_diag + y_off).reshape(b, s, SSM_GROUPS, SSM_HPG, SSM_HEAD_DIM)


def diff_attention(q, k, v, lam):
    s = q.shape[1]
    scale = ATTN_HEAD_DIM ** -0.5
    outs = []
    for i in range(s // Q_BLOCK):
        q0 = i * Q_BLOCK
        kv = q0 + Q_BLOCK
        qb = q[:, q0:kv]
        scores = jnp.einsum("bqhmd,bkhmd->bhmqk", qb, k[:, :kv]).astype(jnp.float32) * scale
        mask = jnp.arange(kv)[None, :] <= (q0 + jnp.arange(Q_BLOCK))[:, None]
        probs = jax.nn.softmax(jnp.where(mask, scores, -jnp.inf), axis=-1)
        attn = probs[:, :, 0] - lam * probs[:, :, 1]
        outs.append(jnp.einsum("bhqk,bkhe->bqhe", attn.astype(v.dtype), v[:, :kv]))
    return jnp.concatenate(outs, axis=1)


def hybrid_layer(x, layer, pre_norm_w, post_norm_w, w_in, conv_w, conv_b, dt_bias, a_log,
                 d_skip, ssm_norm_w, lambda_q1, lambda_k1, lambda_q2, lambda_k2,
                 attn_subln_w, w_out):
    b, s, _ = x.shape
    h = rms_norm(x, pre_norm_w)
    proj = h @ w_in.astype(h.dtype)
    i0 = D_MIX
    i1 = i0 + CONV_CH
    i2 = i1 + SSM_HEADS
    i3 = i2 + D_ATTN
    i4 = i3 + D_ATTN
    z, xbc, dt_raw, q, k, v = jnp.split(proj, [i0, i1, i2, i3, i4], axis=-1)
    z_ssm, z_attn = jnp.split(z, [D_SSM], axis=-1)

    xbc = jax.nn.silu(causal_depthwise_conv(xbc, conv_w, conv_b)).astype(jnp.float32)
    xs, bs, cs = jnp.split(xbc, [D_SSM, D_SSM + SSM_GROUPS * SSM_STATE], axis=-1)
    dt = jax.nn.softplus(dt_raw.astype(jnp.float32) + dt_bias.astype(jnp.float32))
    a_head = -jnp.exp(a_log.astype(jnp.float32)).reshape(SSM_GROUPS, SSM_HPG)
    xh = xs.reshape(b, s, SSM_GROUPS, SSM_HPG, SSM_HEAD_DIM)
    y = ssd_chunked(xh, dt.reshape(b, s, SSM_GROUPS, SSM_HPG), a_head,
                    bs.reshape(b, s, SSM_GROUPS, SSM_STATE), cs.reshape(b, s, SSM_GROUPS, SSM_STATE))
    y = y + d_skip.astype(jnp.float32).reshape(SSM_GROUPS, SSM_HPG)[..., None] * xh
    y = y.reshape(b, s, D_SSM).astype(x.dtype) * jax.nn.silu(z_ssm)
    gs = D_SSM // SSM_GROUPS
    y_ssm = rms_norm(y.reshape(b, s, SSM_GROUPS, gs), ssm_norm_w.reshape(SSM_GROUPS, gs))
    y_ssm = y_ssm.reshape(b, s, D_SSM)

    lam_init = 0.8 - 0.6 * math.exp(-0.3 * layer)
    lam = (jnp.exp(jnp.sum(lambda_q1.astype(jnp.float32) * lambda_k1.astype(jnp.float32)))
           - jnp.exp(jnp.sum(lambda_q2.astype(jnp.float32) * lambda_k2.astype(jnp.float32)))
           + lam_init)
    qh = q.reshape(b, s, ATTN_HEADS, 2, ATTN_HEAD_DIM)
    kh = k.reshape(b, s, ATTN_HEADS, 2, ATTN_HEAD_DIM)
    vh = v.reshape(b, s, ATTN_HEADS, ATTN_VDIM)
    o = diff_attention(qh, kh, vh, lam)
    o = rms_norm(o, attn_subln_w) * (1.0 - lam_init)
    y_attn = o.reshape(b, s, D_ATTN) * jax.nn.silu(z_attn)

    out = jnp.concatenate([y_ssm, y_attn], axis=-1) @ w_out.astype(x.dtype)
    return x + rms_norm(out, post_norm_w)


def setup_inputs(seed: int = 0) -> dict:
    key = jax.random.key(seed)
    ks = jax.random.split(key, 20)
    f32 = jnp.float32
    x = jax.random.normal(ks[0], (BATCH, SEQ, D_MODEL), f32)
    pre_norm_w = 1.0 + 0.02 * jax.random.normal(ks[1], (DEPTH, D_MODEL), f32)
    post_norm_w = 1.0 + 0.02 * jax.random.normal(ks[2], (DEPTH, D_MODEL), f32)
    w_in = jax.random.normal(ks[3], (DEPTH, D_MODEL, IN_COLS), f32) * D_MODEL ** -0.5
    conv_w = jax.random.normal(ks[4], (DEPTH, CONV_WIDTH, CONV_CH), f32) * CONV_WIDTH ** -0.5
    conv_b = 0.02 * jax.random.normal(ks[5], (DEPTH, CONV_CH), f32)
    dt0 = jnp.exp(jax.random.uniform(ks[6], (DEPTH, SSM_HEADS), f32,
                                     math.log(1e-3), math.log(1e-1)))
    dt_bias = dt0 + jnp.log(-jnp.expm1(-dt0))
    a_log = jnp.log(jax.random.uniform(ks[7], (DEPTH, SSM_HEADS), f32, 1.0, 16.0))
    d_skip = 1.0 + 0.1 * jax.random.normal(ks[8], (DEPTH, SSM_HEADS), f32)
    ssm_norm_w = 1.0 + 0.02 * jax.random.normal(ks[9], (DEPTH, D_SSM), f32)
    lambda_q1 = 0.1 * jax.random.normal(ks[10], (DEPTH, ATTN_HEAD_DIM), f32)
    lambda_k1 = 0.1 * jax.random.normal(ks[11], (DEPTH, ATTN_HEAD_DIM), f32)
    lambda_q2 = 0.1 * jax.random.normal(ks[12], (DEPTH, ATTN_HEAD_DIM), f32)
    lambda_k2 = 0.1 * jax.random.normal(ks[13], (DEPTH, ATTN_HEAD_DIM), f32)
    attn_subln_w = 1.0 + 0.02 * jax.random.normal(ks[14], (DEPTH, ATTN_VDIM), f32)
    w_out = jax.random.normal(ks[15], (DEPTH, D_MIX, D_MODEL), f32) * D_MIX ** -0.5
    return {"x": x, "pre_norm_w": pre_norm_w, "post_norm_w": post_norm_w, "w_in": w_in,
            "conv_w": conv_w, "conv_b": conv_b, "dt_bias": dt_bias, "a_log": a_log,
            "d_skip": d_skip, "ssm_norm_w": ssm_norm_w, "lambda_q1": lambda_q1,
            "lambda_k1": lambda_k1, "lambda_q2": lambda_q2, "lambda_k2": lambda_k2,
            "attn_subln_w": attn_subln_w, "w_out": w_out}


def reference(x, pre_norm_w, post_norm_w, w_in, conv_w, conv_b, dt_bias, a_log, d_skip,
              ssm_norm_w, lambda_q1, lambda_k1, lambda_q2, lambda_k2, attn_subln_w, w_out):
    for layer in range(DEPTH):
        x = hybrid_layer(x, layer, pre_norm_w[layer], post_norm_w[layer], w_in[layer],
                         conv_w[layer], conv_b[layer], dt_bias[layer], a_log[layer],
                         d_skip[layer], ssm_norm_w[layer], lambda_q1[layer], lambda_k1[layer],
                         lambda_q2[layer], lambda_k2[layer], attn_subln_w[layer], w_out[layer])
    return x
```

```python
import functools
import math

import jax
import jax.numpy as jnp
from jax import lax
from jax.experimental import pallas as pl
from jax.experimental.pallas import tpu as pltpu

F32 = jnp.float32
BF16 = jnp.bfloat16

D_MODEL = 4096
D_SSM = 2048
D_ATTN = 2048
SSM_HEAD_DIM = 64
SSM_HEADS = 32
SSM_GROUPS = 4
SSM_HPG = 8
SSM_STATE = 128
CONV_WIDTH = 4
CHUNK = 128
ATTN_HEAD_DIM = 64
ATTN_VDIM = 128
ATTN_HEADS = 16
NORM_EPS = 1e-6
GROUP_W = D_SSM // SSM_GROUPS
BC_W = 2 * SSM_GROUPS * SSM_STATE

LANES = 128
SUBLANES = 8
VMEM_LIMIT = 56 * 1024 * 1024

OFF_ZS = 0
OFF_XS = 2048
OFF_ZA = 4096
OFF_Q = 6144
OFF_K = 8192
OFF_V = 10240
OFF_BC = 12288
PROJ_W = 13312
DT_PAD = LANES

IN_TM, IN_TN = 512, 1024
AT_TQ = 256
OUT_TM, OUT_TN = 512, 512


def _silu(x):
    return x * (1.0 / (1.0 + jnp.exp(-x)))


def _in_proj_kernel(x_ref, nw_ref, w_ref, wdt_ref, o_ref, dt_ref, h_scr):
    j = pl.program_id(1)

    @pl.when(j == 0)
    def _():
        x = x_ref[...]
        ms = jnp.mean(x * x, axis=-1, keepdims=True)
        h = (x * lax.rsqrt(ms + NORM_EPS)) * nw_ref[...]
        hb = h.astype(BF16)
        h_scr[...] = hb
        dt_ref[...] = jnp.dot(hb, wdt_ref[...], preferred_element_type=F32)

    o_ref[...] = jnp.dot(h_scr[...], w_ref[...],
                         preferred_element_type=F32).astype(o_ref.dtype)


def _in_proj(x2, norm_w, w_main, w_dt):
    m = x2.shape[0]
    return pl.pallas_call(
        _in_proj_kernel,
        grid=(m // IN_TM, PROJ_W // IN_TN),
        in_specs=[
            pl.BlockSpec((IN_TM, D_MODEL), lambda i, j: (i, 0)),
            pl.BlockSpec((1, D_MODEL), lambda i, j: (0, 0)),
            pl.BlockSpec((D_MODEL, IN_TN), lambda i, j: (0, j)),
            pl.BlockSpec((D_MODEL, DT_PAD), lambda i, j: (0, 0)),
        ],
        out_specs=[
            pl.BlockSpec((IN_TM, IN_TN), lambda i, j: (i, j)),
            pl.BlockSpec((IN_TM, DT_PAD), lambda i, j: (i, 0)),
        ],
        out_shape=[
            jax.ShapeDtypeStruct((m, PROJ_W), BF16),
            jax.ShapeDtypeStruct((m, DT_PAD), F32),
        ],
        scratch_shapes=[pltpu.VMEM((IN_TM, D_MODEL), BF16)],
        compiler_params=pltpu.CompilerParams(
            dimension_semantics=("parallel", "arbitrary"),
            vmem_limit_bytes=VMEM_LIMIT),
        name="in_proj",
    )(x2, norm_w, w_main, w_dt)


def _split_bf16(x, terms):
    parts = []
    r = x
    for _ in range(terms):
        p = r.astype(BF16)
        parts.append(p)
        r = r - p.astype(F32)
    return parts


def _ssd_kernel(xs_ref, bc_ref, dt_ref, z_ref, cwx_ref, cwb_ref, cbx_ref, cbb_ref,
                dtb_ref, alog_ref, dsk_ref, nw_ref, y_ref, extx, extb, state):
    c = pl.program_id(1)
    L = CHUNK
    H0 = SUBLANES

    @pl.when(c == 0)
    def _():
        extx[0:H0, :] = jnp.zeros((H0, D_SSM), F32)
        extb[0:H0, :] = jnp.zeros((H0, BC_W), F32)
        state[...] = jnp.zeros(state.shape, F32)

    extx[H0:H0 + L, :] = xs_ref[...].astype(F32)
    extb[H0:H0 + L, :] = bc_ref[...].astype(F32)

    def conv(ext, w_ref, b_ref):
        acc = b_ref[...] + w_ref[CONV_WIDTH - 1:CONV_WIDTH, :] * ext[H0:H0 + L, :]
        for w in range(CONV_WIDTH - 1):
            lo = H0 - (CONV_WIDTH - 1) + w
            acc = acc + w_ref[w:w + 1, :] * ext[lo:lo + L, :]
        return _silu(acc)

    xs = conv(extx, cwx_ref, cbx_ref)
    bc = conv(extb, cwb_ref, cbb_ref)
    extx[0:H0, :] = extx[L:L + H0, :]
    extb[0:H0, :] = extb[L:L + H0, :]

    dtr = dt_ref[...] + dtb_ref[...]
    dt = jnp.maximum(dtr, 0.0) + jnp.log1p(jnp.exp(-jnp.abs(dtr)))
    a = dt * (-jnp.exp(alog_ref[...]))

    row = lax.broadcasted_iota(jnp.int32, (L, L), 0)
    col = lax.broadcasted_iota(jnp.int32, (L, L), 1)
    tril = row >= col
    tri = jnp.where(tril, 1.0, 0.0).astype(BF16)
    a3 = jnp.concatenate(_split_bf16(a, 3), axis=1)
    cs3 = jnp.dot(tri, a3, preferred_element_type=F32)
    cs = (cs3[:, 0:LANES] + cs3[:, LANES:2 * LANES]) + cs3[:, 2 * LANES:3 * LANES]
    cs_t = cs.T

    er = lax.broadcasted_iota(jnp.int32, (LANES, D_SSM), 0)
    ec = lax.broadcasted_iota(jnp.int32, (LANES, D_SSM), 1)
    expand = jnp.where(ec // SSM_HEAD_DIM == er, 1.0, 0.0).astype(BF16)
    stack = jnp.concatenate(_split_bf16(dt, 2) + _split_bf16(cs, 2), axis=0)
    ex = jnp.dot(stack, expand, preferred_element_type=F32)
    dt_x = ex[0:L] + ex[L:2 * L]
    cs_x = ex[2 * L:3 * L] + ex[3 * L:4 * L]
    cs_last = cs_x[L - 1:L, :]

    xdt = xs * dt_x
    xdt_b = xdt.astype(BF16)
    xdec_b = (xdt * jnp.exp(cs_last - cs_x)).astype(BF16)
    dec_out = jnp.exp(cs_x)
    dec_chunk = jnp.exp(cs_last)

    lane = lax.broadcasted_iota(jnp.int32, (L, LANES), 1)
    low_half = lane < SSM_HEAD_DIM
    neg_inf = jnp.float32(-jnp.inf)

    y_parts = []
    for g in range(SSM_GROUPS):
        gsl = slice(g * GROUP_W, (g + 1) * GROUP_W)
        b_g = bc[:, g * SSM_STATE:(g + 1) * SSM_STATE].astype(BF16)
        c_g = bc[:, BC_W // 2 + g * SSM_STATE:BC_W // 2 + (g + 1) * SSM_STATE].astype(BF16)
        cb = lax.dot_general(c_g, b_g, (((1,), (1,)), ((), ())),
                             preferred_element_type=F32)
        s_in = state[g]
        y_off = jnp.dot(c_g, s_in.astype(BF16), preferred_element_type=F32) * dec_out[:, gsl]
        s_new = lax.dot_general(b_g, xdec_b[:, gsl], (((0,), (0,)), ((), ())),
                                preferred_element_type=F32)
        state[g] = s_in * dec_chunk[:, gsl] + s_new

        pair_out = []
        for pr in range(SSM_HPG // 2):
            ms = []
            for r in (2 * pr, 2 * pr + 1):
                h = g * SSM_HPG + r
                diff = cs[:, h:h + 1] - cs_t[h:h + 1, :]
                decay = jnp.exp(jnp.where(tril, diff, neg_inf))
                ms.append((cb * decay).astype(BF16))
            m2 = jnp.concatenate(ms, axis=1)
            c0 = g * GROUP_W + pr * LANES
            xp = xdt_b[:, c0:c0 + LANES]
            zero = jnp.zeros_like(xp)
            rhs = jnp.concatenate([jnp.where(low_half, xp, zero),
                                   jnp.where(low_half, zero, xp)], axis=0)
            pair_out.append(jnp.dot(m2, rhs, preferred_element_type=F32))
        y_parts.append(jnp.concatenate(pair_out, axis=1) + y_off)
    y = jnp.concatenate(y_parts, axis=1)

    y = y + dsk_ref[...] * xs
    y = y * _silu(z_ref[...].astype(F32))
    outs = []
    for g in range(SSM_GROUPS):
        gsl = slice(g * GROUP_W, (g + 1) * GROUP_W)
        yg = y[:, gsl]
        ms = jnp.mean(yg * yg, axis=-1, keepdims=True)
        outs.append((yg * lax.rsqrt(ms + NORM_EPS)) * nw_ref[:, gsl])
    y_ref[...] = jnp.concatenate(outs, axis=1).astype(y_ref.dtype)


def _ssd(proj3, dt3, cwx, cwb, cbx, cbb, dtb, alog, dsk, nw):
    b, s, _ = proj3.shape
    nc = s // CHUNK
    full = lambda shape: pl.BlockSpec(shape, lambda bi, ci: (0, 0))
    return pl.pallas_call(
        _ssd_kernel,
        grid=(b, nc),
        in_specs=[
            pl.BlockSpec((None, CHUNK, D_SSM), lambda bi, ci: (bi, ci, OFF_XS // D_SSM)),
            pl.BlockSpec((None, CHUNK, BC_W), lambda bi, ci: (bi, ci, OFF_BC // BC_W)),
            pl.BlockSpec((None, CHUNK, DT_PAD), lambda bi, ci: (bi, ci, 0)),
            pl.BlockSpec((None, CHUNK, D_SSM), lambda bi, ci: (bi, ci, OFF_ZS // D_SSM)),
            full((CONV_WIDTH, D_SSM)), full((CONV_WIDTH, BC_W)),
            full((1, D_SSM)), full((1, BC_W)),
            full((1, DT_PAD)), full((1, DT_PAD)),
            full((1, D_SSM)), full((1, D_SSM)),
        ],
        out_specs=pl.BlockSpec((None, CHUNK, D_SSM), lambda bi, ci: (bi, ci, 0)),
        out_shape=jax.ShapeDtypeStruct((b, s, D_SSM), BF16),
        scratch_shapes=[
            pltpu.VMEM((CHUNK + 2 * SUBLANES, D_SSM), F32),
            pltpu.VMEM((CHUNK + 2 * SUBLANES, BC_W), F32),
            pltpu.VMEM((SSM_GROUPS, SSM_STATE, GROUP_W), F32),
        ],
        compiler_params=pltpu.CompilerParams(
            dimension_semantics=("parallel", "arbitrary"),
            vmem_limit_bytes=VMEM_LIMIT),
        name="ssd",
    )(proj3, proj3, dt3, proj3, cwx, cwb, cbx, cbb, dtb, alog, dsk, nw)


def _attn_kernel(lam_init, q_ref, k_ref, v_ref, z_ref, lamv_ref, sw_ref, o_ref,
                 vt_scr, acc1, acc2):
    i = pl.program_id(2)
    t = AT_TQ
    nblk = vt_scr.shape[0]

    @pl.when(i == 0)
    def _():
        for cblk in range(nblk):
            vb = v_ref[cblk * t:(cblk + 1) * t, :].astype(F32)
            vt_scr[cblk] = vb.T.astype(BF16)

    scale = ATTN_HEAD_DIM ** -0.5
    q_t = (q_ref[...].astype(F32) * scale).T
    sub = lax.broadcasted_iota(jnp.int32, q_t.shape, 0)
    q1_t = jnp.where(sub < ATTN_HEAD_DIM, q_t, 0.0).astype(BF16)
    q2_t = jnp.where(sub >= ATTN_HEAD_DIM, q_t, 0.0).astype(BF16)

    acc1[...] = jnp.zeros(acc1.shape, F32)
    acc2[...] = jnp.zeros(acc2.shape, F32)
    neg_inf = jnp.float32(-jnp.inf)

    def one_map(s, vt_b, m, l, acc):
        m_new = jnp.maximum(m, jnp.max(s, axis=0, keepdims=True))
        alpha = jnp.exp(m - m_new)
        p = jnp.exp(s - m_new)
        l_new = alpha * l + jnp.sum(p, axis=0, keepdims=True)
        acc[...] = alpha * acc[...] + jnp.dot(vt_b, p.astype(BF16),
                                              preferred_element_type=F32)
        return m_new, l_new

    def block(j, carry, masked):
        m1, l1, m2, l2 = carry
        start = pl.multiple_of(j * t, t)
        k_b = k_ref[pl.ds(start, t), :]
        vt_b = vt_scr[j]
        s1 = jnp.dot(k_b, q1_t, preferred_element_type=F32)
        s2 = jnp.dot(k_b, q2_t, preferred_element_type=F32)
        if masked:
            kpos = lax.broadcasted_iota(jnp.int32, (t, t), 0)
            qpos = lax.broadcasted_iota(jnp.int32, (t, t), 1)
            keep = kpos <= qpos
            s1 = jnp.where(keep, s1, neg_inf)
            s2 = jnp.where(keep, s2, neg_inf)
        m1, l1 = one_map(s1, vt_b, m1, l1, acc1)
        m2, l2 = one_map(s2, vt_b, m2, l2, acc2)
        return m1, l1, m2, l2

    init = (jnp.full((1, t), neg_inf, F32), jnp.zeros((1, t), F32),
            jnp.full((1, t), neg_inf, F32), jnp.zeros((1, t), F32))
    carry = lax.fori_loop(0, i, lambda j, cr: block(j, cr, False), init)
    m1, l1, m2, l2 = block(i, carry, True)

    lv = lamv_ref[...]
    lam = (jnp.exp(jnp.sum(lv[0:1] * lv[1:2], axis=-1, keepdims=True))
           - jnp.exp(jnp.sum(lv[2:3] * lv[3:4], axis=-1, keepdims=True))
           + lam_init)
    o_t = acc1[...] * (1.0 / l1) - lam * (acc2[...] * (1.0 / l2))
    o = o_t.T
    ms = jnp.mean(o * o, axis=-1, keepdims=True)
    on = (o * lax.rsqrt(ms + NORM_EPS)) * sw_ref[...] * (1.0 - lam_init)
    o_ref[...] = (on * _silu(z_ref[...].astype(F32))).astype(o_ref.dtype)


def _diff_attn(proj3, lamv, subln_w, lam_init):
    b, s, _ = proj3.shape
    nq = s // AT_TQ
    hb = lambda off: off // ATTN_VDIM
    return pl.pallas_call(
        functools.partial(_attn_kernel, lam_init),
        grid=(b, ATTN_HEADS, nq),
        in_specs=[
            pl.BlockSpec((None, AT_TQ, ATTN_VDIM), lambda bi, h, i: (bi, i, hb(OFF_Q) + h)),
            pl.BlockSpec((None, s, ATTN_VDIM), lambda bi, h, i: (bi, 0, hb(OFF_K) + h)),
            pl.BlockSpec((None, s, ATTN_VDIM), lambda bi, h, i: (bi, 0, hb(OFF_V) + h)),
            pl.BlockSpec((None, AT_TQ, ATTN_VDIM), lambda bi, h, i: (bi, i, hb(OFF_ZA) + h)),
            pl.BlockSpec((4, ATTN_HEAD_DIM), lambda bi, h, i: (0, 0)),
            pl.BlockSpec((1, ATTN_VDIM), lambda bi, h, i: (0, 0)),
        ],
        out_specs=pl.BlockSpec((None, AT_TQ, ATTN_VDIM), lambda bi, h, i: (bi, i, h)),
        out_shape=jax.ShapeDtypeStruct((b, s, D_ATTN), BF16),
        scratch_shapes=[
            pltpu.VMEM((nq, ATTN_VDIM, AT_TQ), BF16),
            pltpu.VMEM((ATTN_VDIM, AT_TQ), F32),
            pltpu.VMEM((ATTN_VDIM, AT_TQ), F32),
        ],
        compiler_params=pltpu.CompilerParams(
            dimension_semantics=("parallel", "parallel", "arbitrary"),
            vmem_limit_bytes=VMEM_LIMIT),
        name="diffattn",
    )(proj3, proj3, proj3, proj3, lamv, subln_w)


def _out_proj_kernel(ys_ref, ya_ref, ws_ref, wa_ref, x_ref, nw_ref, o_ref, ssq):
    j = pl.program_id(1)
    nj = pl.num_programs(1)

    @pl.when(j == 0)
    def _():
        ssq[...] = jnp.zeros(ssq.shape, F32)

    acc = (jnp.dot(ys_ref[...], ws_ref[...], preferred_element_type=F32)
           + jnp.dot(ya_ref[...], wa_ref[...], preferred_element_type=F32))
    ssq[...] += jnp.sum(acc * acc, axis=-1, keepdims=True)
    for jj in range(D_MODEL // OUT_TN):
        @pl.when(j == jj)
        def _(jj=jj):
            o_ref[:, jj * OUT_TN:(jj + 1) * OUT_TN] = acc

    @pl.when(j == nj - 1)
    def _():
        inv = lax.rsqrt(ssq[...] * (1.0 / D_MODEL) + NORM_EPS)
        for jj in range(D_MODEL // OUT_TN):
            sl = slice(jj * OUT_TN, (jj + 1) * OUT_TN)
            o_ref[:, sl] = x_ref[:, sl] + (o_ref[:, sl] * inv) * nw_ref[:, sl]


def _out_proj(ys, ya, w_s, w_a, x2, norm_w):
    m = x2.shape[0]
    return pl.pallas_call(
        _out_proj_kernel,
        grid=(m // OUT_TM, D_MODEL // OUT_TN),
        in_specs=[
            pl.BlockSpec((OUT_TM, D_SSM), lambda i, j: (i, 0)),
            pl.BlockSpec((OUT_TM, D_ATTN), lambda i, j: (i, 0)),
            pl.BlockSpec((D_SSM, OUT_TN), lambda i, j: (0, j)),
            pl.BlockSpec((D_ATTN, OUT_TN), lambda i, j: (0, j)),
            pl.BlockSpec((OUT_TM, D_MODEL), lambda i, j: (i, 0)),
            pl.BlockSpec((1, D_MODEL), lambda i, j: (0, 0)),
        ],
        out_specs=pl.BlockSpec((OUT_TM, D_MODEL), lambda i, j: (i, 0)),
        out_shape=jax.ShapeDtypeStruct((m, D_MODEL), F32),
        scratch_shapes=[pltpu.VMEM((OUT_TM, 1), F32)],
        compiler_params=pltpu.CompilerParams(
            dimension_semantics=("parallel", "arbitrary"),
            vmem_limit_bytes=VMEM_LIMIT),
        name="out_proj",
    )(ys, ya, w_s, w_a, x2, norm_w)


def _layer(x, layer, pre_norm_w, post_norm_w, w_in, conv_w, conv_b, dt_bias, a_log, d_skip,
           ssm_norm_w, lambda_q1, lambda_k1, lambda_q2, lambda_k2, attn_subln_w, w_out):
    b, s, d = x.shape
    x2 = x.reshape(b * s, d)

    i_x = 4096
    i_dt = i_x + D_SSM + BC_W
    i_q = i_dt + SSM_HEADS
    w_main = jnp.concatenate([
        w_in[:, 0:D_SSM], w_in[:, i_x:i_x + D_SSM], w_in[:, D_SSM:4096],
        w_in[:, i_q:i_q + 3 * D_ATTN], w_in[:, i_x + D_SSM:i_dt]], axis=1).astype(BF16)
    w_dt = jnp.pad(w_in[:, i_dt:i_q], ((0, 0), (0, DT_PAD - SSM_HEADS))).astype(BF16)

    proj, dt_raw = _in_proj(x2, pre_norm_w.reshape(1, d), w_main, w_dt)
    proj3 = proj.reshape(b, s, PROJ_W)
    dt3 = dt_raw.reshape(b, s, DT_PAD)

    pad_h = lambda v: jnp.pad(v.reshape(1, SSM_HEADS), ((0, 0), (0, DT_PAD - SSM_HEADS)))
    y_ssm = _ssd(
        proj3, dt3,
        conv_w[:, :D_SSM], conv_w[:, D_SSM:],
        conv_b[:D_SSM].reshape(1, D_SSM), conv_b[D_SSM:].reshape(1, BC_W),
        pad_h(dt_bias), pad_h(a_log),
        jnp.repeat(d_skip, SSM_HEAD_DIM).reshape(1, D_SSM),
        ssm_norm_w.reshape(1, D_SSM))

    lam_init = 0.8 - 0.6 * math.exp(-0.3 * layer)
    lamv = jnp.stack([lambda_q1, lambda_k1, lambda_q2, lambda_k2], axis=0)
    y_attn = _diff_attn(proj3, lamv, attn_subln_w.reshape(1, ATTN_VDIM), lam_init)

    w_o = w_out.astype(BF16)
    out = _out_proj(y_ssm.reshape(b * s, D_SSM), y_attn.reshape(b * s, D_ATTN),
                    w_o[:D_SSM], w_o[D_SSM:], x2, post_norm_w.reshape(1, d))
    return out.reshape(b, s, d)


def kernel(x, pre_norm_w, post_norm_w, w_in, conv_w, conv_b, dt_bias, a_log, d_skip, ssm_norm_w, lambda_q1, lambda_k1, lambda_q2, lambda_k2, attn_subln_w, w_out):
    for layer in range(w_in.shape[0]):
        x = _layer(x, layer, pre_norm_w[layer], post_norm_w[layer], w_in[layer],
                   conv_w[layer], conv_b[layer], dt_bias[layer], a_log[layer], d_skip[layer],
                   ssm_norm_w[layer], lambda_q1[layer], lambda_k1[layer], lambda_q2[layer],
                   lambda_k2[layer], attn_subln_w[layer], w_out[layer])
    return x
```

```python
import functools
import math

import jax
import jax.numpy as jnp
from jax import lax
from jax.experimental import pallas as pl
from jax.experimental.pallas import tpu as pltpu

F32 = jnp.float32
BF16 = jnp.bfloat16

D_MODEL = 4096
D_SSM = 2048
D_ATTN = 2048
SSM_HEAD_DIM = 64
SSM_HEADS = 32
SSM_GROUPS = 4
SSM_HPG = 8
SSM_STATE = 128
CONV_WIDTH = 4
CHUNK = 128
ATTN_HEAD_DIM = 64
ATTN_VDIM = 128
ATTN_HEADS = 16
NORM_EPS = 1e-6
GROUP_W = D_SSM // SSM_GROUPS
BC_W = 2 * SSM_GROUPS * SSM_STATE

LANES = 128
SUBLANES = 8
VMEM_LIMIT = 56 * 1024 * 1024

OFF_ZS = 0
OFF_XS = 2048
OFF_ZA = 4096
OFF_Q = 6144
OFF_K = 8192
OFF_V = 10240
OFF_BC = 12288
PROJ_W = 13312
DT_PAD = LANES

IN_TM, IN_TN = 512, 1024
AT_TQ, AT_TK = 512, 512
BF16_SUBLANES = 16
AT_ACC_ROWS = ATTN_VDIM + BF16_SUBLANES
OUT_TM, OUT_TN = 512, 512


def _silu(x):
    return x * (1.0 / (1.0 + jnp.exp(-x)))


def _in_proj_kernel(x_ref, nw_ref, w_ref, wdt_ref, cs_ref, o_ref, dt_ref, h_scr):
    j = pl.program_id(1)

    @pl.when(j == 0)
    def _():
        x = x_ref[...]
        ms = jnp.mean(x * x, axis=-1, keepdims=True)
        h = (x * lax.rsqrt(ms + NORM_EPS)) * nw_ref[...]
        hb = h.astype(BF16)
        h_scr[...] = hb
        dt_ref[...] = jnp.dot(hb, wdt_ref[...], preferred_element_type=F32)

    acc = jnp.dot(h_scr[...], w_ref[...], preferred_element_type=F32)
    o_ref[...] = (acc * cs_ref[...]).astype(o_ref.dtype)


def _in_proj(x2, norm_w, w_main, w_dt, col_scale):
    m = x2.shape[0]
    return pl.pallas_call(
        _in_proj_kernel,
        grid=(m // IN_TM, PROJ_W // IN_TN),
        in_specs=[
            pl.BlockSpec((IN_TM, D_MODEL), lambda i, j: (i, 0)),
            pl.BlockSpec((1, D_MODEL), lambda i, j: (0, 0)),
            pl.BlockSpec((D_MODEL, IN_TN), lambda i, j: (0, j)),
            pl.BlockSpec((D_MODEL, DT_PAD), lambda i, j: (0, 0)),
            pl.BlockSpec((1, IN_TN), lambda i, j: (0, j)),
        ],
        out_specs=[
            pl.BlockSpec((IN_TM, IN_TN), lambda i, j: (i, j)),
            pl.BlockSpec((IN_TM, DT_PAD), lambda i, j: (i, 0)),
        ],
        out_shape=[
            jax.ShapeDtypeStruct((m, PROJ_W), BF16),
            jax.ShapeDtypeStruct((m, DT_PAD), F32),
        ],
        scratch_shapes=[pltpu.VMEM((IN_TM, D_MODEL), BF16)],
        compiler_params=pltpu.CompilerParams(
            dimension_semantics=("parallel", "arbitrary"),
            vmem_limit_bytes=VMEM_LIMIT),
        name="in_proj",
    )(x2, norm_w, w_main, w_dt, col_scale)


def _split_bf16(x, terms):
    parts = []
    r = x
    for _ in range(terms):
        p = r.astype(BF16)
        parts.append(p)
        r = r - p.astype(F32)
    return parts


def _ssd_kernel(xs_ref, bc_ref, dt_ref, z_ref, cwx_ref, cwb_ref, cbx_ref, cbb_ref,
                dtb_ref, alog_ref, dsk_ref, nw_ref, y_ref, extx, extb, state):
    c = pl.program_id(1)
    L = CHUNK
    H0 = SUBLANES

    @pl.when(c == 0)
    def _():
        extx[0:H0, :] = jnp.zeros((H0, D_SSM), F32)
        extb[0:H0, :] = jnp.zeros((H0, BC_W), F32)
        state[...] = jnp.zeros(state.shape, F32)

    extx[H0:H0 + L, :] = xs_ref[...].astype(F32)
    extb[H0:H0 + L, :] = bc_ref[...].astype(F32)

    def conv(ext, w_ref, b_ref):
        acc = b_ref[...] + w_ref[CONV_WIDTH - 1:CONV_WIDTH, :] * ext[H0:H0 + L, :]
        for w in range(CONV_WIDTH - 1):
            lo = H0 - (CONV_WIDTH - 1) + w
            acc = acc + w_ref[w:w + 1, :] * ext[lo:lo + L, :]
        return _silu(acc)

    xs = conv(extx, cwx_ref, cbx_ref)
    bc = conv(extb, cwb_ref, cbb_ref)
    extx[0:H0, :] = extx[L:L + H0, :]
    extb[0:H0, :] = extb[L:L + H0, :]

    dtr = dt_ref[...] + dtb_ref[...]
    dt = jnp.maximum(dtr, 0.0) + jnp.log1p(jnp.exp(-jnp.abs(dtr)))
    a = dt * (-jnp.exp(alog_ref[...]))

    row = lax.broadcasted_iota(jnp.int32, (L, L), 0)
    col = lax.broadcasted_iota(jnp.int32, (L, L), 1)
    tril = row >= col
    tri = jnp.where(tril, 1.0, 0.0).astype(BF16)
    a3 = jnp.concatenate(_split_bf16(a, 3), axis=1)
    cs3 = jnp.dot(tri, a3, preferred_element_type=F32)
    cs = (cs3[:, 0:LANES] + cs3[:, LANES:2 * LANES]) + cs3[:, 2 * LANES:3 * LANES]
    cs_t = cs.T

    er = lax.broadcasted_iota(jnp.int32, (LANES, D_SSM), 0)
    ec = lax.broadcasted_iota(jnp.int32, (LANES, D_SSM), 1)
    expand = jnp.where(ec // SSM_HEAD_DIM == er, 1.0, 0.0).astype(BF16)
    stack = jnp.concatenate(_split_bf16(dt, 2) + _split_bf16(cs, 2), axis=0)
    ex = jnp.dot(stack, expand, preferred_element_type=F32)
    dt_x = ex[0:L] + ex[L:2 * L]
    cs_x = ex[2 * L:3 * L] + ex[3 * L:4 * L]
    cs_last = cs_x[L - 1:L, :]

    xdt = xs * dt_x
    xdt_b = xdt.astype(BF16)
    xdec_b = (xdt * jnp.exp(cs_last - cs_x)).astype(BF16)
    dec_out = jnp.exp(cs_x)
    dec_chunk = jnp.exp(cs_last)

    lane = lax.broadcasted_iota(jnp.int32, (L, LANES), 1)
    low_half = lane < SSM_HEAD_DIM
    neg_inf = jnp.float32(-jnp.inf)

    y_parts = []
    for g in range(SSM_GROUPS):
        gsl = slice(g * GROUP_W, (g + 1) * GROUP_W)
        b_g = bc[:, g * SSM_STATE:(g + 1) * SSM_STATE].astype(BF16)
        c_g = bc[:, BC_W // 2 + g * SSM_STATE:BC_W // 2 + (g + 1) * SSM_STATE].astype(BF16)
        cb = lax.dot_general(c_g, b_g, (((1,), (1,)), ((), ())),
                             preferred_element_type=F32)
        s_in = state[g]
        y_off = jnp.dot(c_g, s_in.astype(BF16), preferred_element_type=F32) * dec_out[:, gsl]
        s_new = lax.dot_general(b_g, xdec_b[:, gsl], (((0,), (0,)), ((), ())),
                                preferred_element_type=F32)
        state[g] = s_in * dec_chunk[:, gsl] + s_new

        pair_out = []
        for pr in range(SSM_HPG // 2):
            ms = []
            for r in (2 * pr, 2 * pr + 1):
                h = g * SSM_HPG + r
                diff = cs[:, h:h + 1] - cs_t[h:h + 1, :]
                decay = jnp.exp(jnp.where(tril, diff, neg_inf))
                ms.append((cb * decay).astype(BF16))
            m2 = jnp.concatenate(ms, axis=1)
            c0 = g * GROUP_W + pr * LANES
            xp = xdt_b[:, c0:c0 + LANES]
            zero = jnp.zeros_like(xp)
            rhs = jnp.concatenate([jnp.where(low_half, xp, zero),
                                   jnp.where(low_half, zero, xp)], axis=0)
            pair_out.append(jnp.dot(m2, rhs, preferred_element_type=F32))
        y_parts.append(jnp.concatenate(pair_out, axis=1) + y_off)
    y = jnp.concatenate(y_parts, axis=1)

    y = y + dsk_ref[...] * xs
    y = y * _silu(z_ref[...].astype(F32))
    outs = []
    for g in range(SSM_GROUPS):
        gsl = slice(g * GROUP_W, (g + 1) * GROUP_W)
        yg = y[:, gsl]
        ms = jnp.mean(yg * yg, axis=-1, keepdims=True)
        outs.append((yg * lax.rsqrt(ms + NORM_EPS)) * nw_ref[:, gsl])
    y_ref[...] = jnp.concatenate(outs, axis=1).astype(y_ref.dtype)


def _ssd(proj3, dt3, cwx, cwb, cbx, cbb, dtb, alog, dsk, nw):
    b, s, _ = proj3.shape
    nc = s // CHUNK
    full = lambda shape: pl.BlockSpec(shape, lambda bi, ci: (0, 0))
    return pl.pallas_call(
        _ssd_kernel,
        grid=(b, nc),
        in_specs=[
            pl.BlockSpec((None, CHUNK, D_SSM), lambda bi, ci: (bi, ci, OFF_XS // D_SSM)),
            pl.BlockSpec((None, CHUNK, BC_W), lambda bi, ci: (bi, ci, OFF_BC // BC_W)),
            pl.BlockSpec((None, CHUNK, DT_PAD), lambda bi, ci: (bi, ci, 0)),
            pl.BlockSpec((None, CHUNK, D_SSM), lambda bi, ci: (bi, ci, OFF_ZS // D_SSM)),
            full((CONV_WIDTH, D_SSM)), full((CONV_WIDTH, BC_W)),
            full((1, D_SSM)), full((1, BC_W)),
            full((1, DT_PAD)), full((1, DT_PAD)),
            full((1, D_SSM)), full((1, D_SSM)),
        ],
        out_specs=pl.BlockSpec((None, CHUNK, D_SSM), lambda bi, ci: (bi, ci, 0)),
        out_shape=jax.ShapeDtypeStruct((b, s, D_SSM), BF16),
        scratch_shapes=[
            pltpu.VMEM((CHUNK + 2 * SUBLANES, D_SSM), F32),
            pltpu.VMEM((CHUNK + 2 * SUBLANES, BC_W), F32),
            pltpu.VMEM((SSM_GROUPS, SSM_STATE, GROUP_W), F32),
        ],
        compiler_params=pltpu.CompilerParams(
            dimension_semantics=("parallel", "arbitrary"),
            vmem_limit_bytes=VMEM_LIMIT),
        name="ssd",
    )(proj3, proj3, dt3, proj3, cwx, cwb, cbx, cbb, dtb, alog, dsk, nw)


def _attn_kernel(lam_init, q_ref, k_ref, v_ref, z_ref, lamv_ref, sw_ref, o_ref,
                 vt_scr, s1_scr, s2_scr, p1_scr, p2_scr, acc1, acc2):
    i = pl.program_id(2)
    tq, tk = AT_TQ, AT_TK
    ratio = tq // tk
    nkb = vt_scr.shape[0]

    @pl.when(i == 0)
    def _():
        for cblk in range(nkb):
            vb = v_ref[cblk * tk:(cblk + 1) * tk, :].astype(F32)
            vt_scr[cblk, 0:ATTN_VDIM, :] = vb.T.astype(BF16)
            vt_scr[cblk, ATTN_VDIM:AT_ACC_ROWS, :] = jnp.ones((AT_ACC_ROWS - ATTN_VDIM, tk), BF16)

    q_t = q_ref[...].astype(F32).T
    sub = lax.broadcasted_iota(jnp.int32, q_t.shape, 0)
    q1_t = jnp.where(sub < ATTN_HEAD_DIM, q_t, 0.0).astype(BF16)
    q2_t = jnp.where(sub >= ATTN_HEAD_DIM, q_t, 0.0).astype(BF16)

    acc1[...] = jnp.zeros(acc1.shape, F32)
    acc2[...] = jnp.zeros(acc2.shape, F32)
    p1_scr[...] = jnp.zeros(p1_scr.shape, BF16)
    p2_scr[...] = jnp.zeros(p2_scr.shape, BF16)
    neg_inf = jnp.float32(-jnp.inf)

    def scores(j):
        start = pl.multiple_of(j * tk, tk)
        k_b = k_ref[pl.ds(start, tk), :]
        s1_scr[...] = jnp.dot(k_b, q1_t, preferred_element_type=F32)
        s2_scr[...] = jnp.dot(k_b, q2_t, preferred_element_type=F32)

    def softmax(j, m, s_scr, p_scr, masked):
        s = s_scr[...]
        if masked:
            kpos = j * tk + lax.broadcasted_iota(jnp.int32, (tk, tq), 0)
            qpos = i * tq + lax.broadcasted_iota(jnp.int32, (tk, tq), 1)
            s = jnp.where(kpos <= qpos, s, neg_inf)
        m_new = jnp.maximum(m, jnp.max(s, axis=0, keepdims=True))
        p_scr[...] = jnp.exp2(s - m_new).astype(BF16)
        return m_new, jnp.exp2(m - m_new)

    def accumulate(j, a1, a2):
        vt_b = vt_scr[jnp.maximum(j, 0)]
        acc1[...] = a1 * acc1[...] + jnp.dot(vt_b, p1_scr[...], preferred_element_type=F32)
        acc2[...] = a2 * acc2[...] + jnp.dot(vt_b, p2_scr[...], preferred_element_type=F32)

    def step(j, carry, masked, prefetch):
        m1, a1, m2, a2 = carry
        accumulate(j - 1, a1, a2)
        m1, a1 = softmax(j, m1, s1_scr, p1_scr, masked)
        m2, a2 = softmax(j, m2, s2_scr, p2_scr, masked)
        if prefetch:
            scores(j + 1)
        return m1, a1, m2, a2

    row = lambda v: jnp.full((1, tq), v, F32)
    carry = (row(neg_inf), row(1.0), row(neg_inf), row(1.0))
    scores(0)
    n_full = i * ratio
    carry = lax.fori_loop(0, n_full, lambda j, cr: step(j, cr, False, True), carry)
    for u in range(ratio):
        carry = step(n_full + u, carry, True, u < ratio - 1)
    _, a1, _, a2 = carry
    accumulate(n_full + ratio - 1, a1, a2)

    lv = lamv_ref[...]
    lam = (jnp.exp(jnp.sum(lv[0:1] * lv[1:2], axis=-1, keepdims=True))
           - jnp.exp(jnp.sum(lv[2:3] * lv[3:4], axis=-1, keepdims=True))
           + lam_init)
    l1 = acc1[ATTN_VDIM:ATTN_VDIM + 1, :]
    l2 = acc2[ATTN_VDIM:ATTN_VDIM + 1, :]
    o_t = (acc1[0:ATTN_VDIM, :] * (1.0 / l1)
           - lam * (acc2[0:ATTN_VDIM, :] * (1.0 / l2)))
    o = o_t.T
    ms = jnp.mean(o * o, axis=-1, keepdims=True)
    on = (o * lax.rsqrt(ms + NORM_EPS)) * sw_ref[...] * (1.0 - lam_init)
    o_ref[...] = (on * _silu(z_ref[...].astype(F32))).astype(o_ref.dtype)


def _diff_attn(proj3, lamv, subln_w, lam_init):
    b, s, _ = proj3.shape
    nq = s // AT_TQ
    hb = lambda off: off // ATTN_VDIM
    return pl.pallas_call(
        functools.partial(_attn_kernel, lam_init),
        grid=(b, ATTN_HEADS, nq),
        in_specs=[
            pl.BlockSpec((None, AT_TQ, ATTN_VDIM), lambda bi, h, i: (bi, i, hb(OFF_Q) + h)),
            pl.BlockSpec((None, s, ATTN_VDIM), lambda bi, h, i: (bi, 0, hb(OFF_K) + h)),
            pl.BlockSpec((None, s, ATTN_VDIM), lambda bi, h, i: (bi, 0, hb(OFF_V) + h)),
            pl.BlockSpec((None, AT_TQ, ATTN_VDIM), lambda bi, h, i: (bi, i, hb(OFF_ZA) + h)),
            pl.BlockSpec((4, ATTN_HEAD_DIM), lambda bi, h, i: (0, 0)),
            pl.BlockSpec((1, ATTN_VDIM), lambda bi, h, i: (0, 0)),
        ],
        out_specs=pl.BlockSpec((None, AT_TQ, ATTN_VDIM), lambda bi, h, i: (bi, i, h)),
        out_shape=jax.ShapeDtypeStruct((b, s, D_ATTN), BF16),
        scratch_shapes=[
            pltpu.VMEM((s // AT_TK, AT_ACC_ROWS, AT_TK), BF16),
            pltpu.VMEM((AT_TK, AT_TQ), F32),
            pltpu.VMEM((AT_TK, AT_TQ), F32),
            pltpu.VMEM((AT_TK, AT_TQ), BF16),
            pltpu.VMEM((AT_TK, AT_TQ), BF16),
            pltpu.VMEM((AT_ACC_ROWS, AT_TQ), F32),
            pltpu.VMEM((AT_ACC_ROWS, AT_TQ), F32),
        ],
        compiler_params=pltpu.CompilerParams(
            dimension_semantics=("parallel", "parallel", "arbitrary"),
            vmem_limit_bytes=VMEM_LIMIT),
        name="diffattn",
    )(proj3, proj3, proj3, proj3, lamv, subln_w)


def _out_proj_kernel(ys_ref, ya_ref, ws_ref, wa_ref, x_ref, nw_ref, o_ref, ssq):
    j = pl.program_id(1)
    nj = pl.num_programs(1)

    @pl.when(j == 0)
    def _():
        ssq[...] = jnp.zeros(ssq.shape, F32)

    acc = (jnp.dot(ys_ref[...], ws_ref[...], preferred_element_type=F32)
           + jnp.dot(ya_ref[...], wa_ref[...], preferred_element_type=F32))
    ssq[...] += jnp.sum(acc * acc, axis=-1, keepdims=True)
    for jj in range(D_MODEL // OUT_TN):
        @pl.when(j == jj)
        def _(jj=jj):
            o_ref[:, jj * OUT_TN:(jj + 1) * OUT_TN] = acc

    @pl.when(j == nj - 1)
    def _():
        inv = lax.rsqrt(ssq[...] * (1.0 / D_MODEL) + NORM_EPS)
        for jj in range(D_MODEL // OUT_TN):
            sl = slice(jj * OUT_TN, (jj + 1) * OUT_TN)
            o_ref[:, sl] = x_ref[:, sl] + (o_ref[:, sl] * inv) * nw_ref[:, sl]


def _out_proj(ys, ya, w_s, w_a, x2, norm_w):
    m = x2.shape[0]
    return pl.pallas_call(
        _out_proj_kernel,
        grid=(m // OUT_TM, D_MODEL // OUT_TN),
        in_specs=[
            pl.BlockSpec((OUT_TM, D_SSM), lambda i, j: (i, 0)),
            pl.BlockSpec((OUT_TM, D_ATTN), lambda i, j: (i, 0)),
            pl.BlockSpec((D_SSM, OUT_TN), lambda i, j: (0, j)),
            pl.BlockSpec((D_ATTN, OUT_TN), lambda i, j: (0, j)),
            pl.BlockSpec((OUT_TM, D_MODEL), lambda i, j: (i, 0)),
            pl.BlockSpec((1, D_MODEL), lambda i, j: (0, 0)),
        ],
        out_specs=pl.BlockSpec((OUT_TM, D_MODEL), lambda i, j: (i, 0)),
        out_shape=jax.ShapeDtypeStruct((m, D_MODEL), F32),
        scratch_shapes=[pltpu.VMEM((OUT_TM, 1), F32)],
        compiler_params=pltpu.CompilerParams(
            dimension_semantics=("parallel", "arbitrary"),
            vmem_limit_bytes=VMEM_LIMIT),
        name="out_proj",
    )(ys, ya, w_s, w_a, x2, norm_w)


def _layer(x, layer, pre_norm_w, post_norm_w, w_in, conv_w, conv_b, dt_bias, a_log, d_skip,
           ssm_norm_w, lambda_q1, lambda_k1, lambda_q2, lambda_k2, attn_subln_w, w_out):
    b, s, d = x.shape
    x2 = x.reshape(b * s, d)

    i_x = 4096
    i_dt = i_x + D_SSM + BC_W
    i_q = i_dt + SSM_HEADS
    w_main = jnp.concatenate([
        w_in[:, 0:D_SSM], w_in[:, i_x:i_x + D_SSM], w_in[:, D_SSM:4096],
        w_in[:, i_q:i_q + 3 * D_ATTN], w_in[:, i_x + D_SSM:i_dt]], axis=1).astype(BF16)
    w_dt = jnp.pad(w_in[:, i_dt:i_q], ((0, 0), (0, DT_PAD - SSM_HEADS))).astype(BF16)

    q_scale = ATTN_HEAD_DIM ** -0.5 * math.log2(math.e)
    col_scale = jnp.ones((1, PROJ_W), F32).at[:, OFF_Q:OFF_Q + D_ATTN].set(q_scale)
    proj, dt_raw = _in_proj(x2, pre_norm_w.reshape(1, d), w_main, w_dt, col_scale)
    proj3 = proj.reshape(b, s, PROJ_W)
    dt3 = dt_raw.reshape(b, s, DT_PAD)

    pad_h = lambda v: jnp.pad(v.reshape(1, SSM_HEADS), ((0, 0), (0, DT_PAD - SSM_HEADS)))
    y_ssm = _ssd(
        proj3, dt3,
        conv_w[:, :D_SSM], conv_w[:, D_SSM:],
        conv_b[:D_SSM].reshape(1, D_SSM), conv_b[D_SSM:].reshape(1, BC_W),
        pad_h(dt_bias), pad_h(a_log),
        jnp.repeat(d_skip, SSM_HEAD_DIM).reshape(1, D_SSM),
        ssm_norm_w.reshape(1, D_SSM))

    lam_init = 0.8 - 0.6 * math.exp(-0.3 * layer)
    lamv = jnp.stack([lambda_q1, lambda_k1, lambda_q2, lambda_k2], axis=0)
    y_attn = _diff_attn(proj3, lamv, attn_subln_w.reshape(1, ATTN_VDIM), lam_init)

    w_o = w_out.astype(BF16)
    out = _out_proj(y_ssm.reshape(b * s, D_SSM), y_attn.reshape(b * s, D_ATTN),
                    w_o[:D_SSM], w_o[D_SSM:], x2, post_norm_w.reshape(1, d))
    return out.reshape(b, s, d)


def kernel(x, pre_norm_w, post_norm_w, w_in, conv_w, conv_b, dt_bias, a_log, d_skip, ssm_norm_w, lambda_q1, lambda_k1, lambda_q2, lambda_k2, attn_subln_w, w_out):
    for layer in range(w_in.shape[0]):
        x = _layer(x, layer, pre_norm_w[layer], post_norm_w[layer], w_in[layer],
                   conv_w[layer], conv_b[layer], dt_bias[layer], a_log[layer], d_skip[layer],
                   ssm_norm_w[layer], lambda_q1[layer], lambda_k1[layer], lambda_q2[layer],
                   lambda_k2[layer], attn_subln_w[layer], w_out[layer])
    return x
```

```python
import functools
import math

import jax
import jax.numpy as jnp
from jax import lax
from jax.experimental import pallas as pl
from jax.experimental.pallas import tpu as pltpu

F32 = jnp.float32
BF16 = jnp.bfloat16

D_MODEL = 4096
D_SSM = 2048
D_ATTN = 2048
SSM_HEAD_DIM = 64
SSM_HEADS = 32
SSM_GROUPS = 4
SSM_HPG = 8
SSM_STATE = 128
CONV_WIDTH = 4
CHUNK = 128
ATTN_HEAD_DIM = 64
ATTN_VDIM = 128
ATTN_HEADS = 16
NORM_EPS = 1e-6
GROUP_W = D_SSM // SSM_GROUPS
BC_W = 2 * SSM_GROUPS * SSM_STATE

LANES = 128
SUBLANES = 8
VMEM_LIMIT = 56 * 1024 * 1024

OFF_ZS = 0
OFF_ZA = 2048
OFF_XS = 4096
OFF_BC = 6144
OFF_Q = 7168
OFF_K = 9216
OFF_V = 11264
PROJ_W = 13312
W_IN_DT = OFF_Q
W_IN_QKV = W_IN_DT + SSM_HEADS
DT_PAD = LANES

IN_TM, IN_TN = 512, 1024
AT_T = 512
AT_HPS = 4
BF16_SUBLANES = 16
AT_ACC_ROWS = ATTN_VDIM + BF16_SUBLANES
OUT_TM, OUT_TN, OUT_RC = 256, 512, 64
OUT_VMEM_LIMIT = 60 * 1024 * 1024


def _silu(x):
    return x * (1.0 / (1.0 + jnp.exp(-x)))


def _in_proj_kernel(x_ref, nw_ref, w_ref, wdt_ref, cs_ref, o_ref, dt_ref, h_scr):
    j = pl.program_id(1)

    @pl.when(j == 0)
    def _():
        x = x_ref[...]
        ms = jnp.mean(x * x, axis=-1, keepdims=True)
        h = (x * lax.rsqrt(ms + NORM_EPS)) * nw_ref[...]
        hb = h.astype(BF16)
        h_scr[...] = hb
        dt_ref[...] = jnp.dot(hb, wdt_ref[...], preferred_element_type=F32)

    acc = jnp.dot(h_scr[...], w_ref[...], preferred_element_type=F32)
    o_ref[...] = (acc * cs_ref[...]).astype(o_ref.dtype)


def _in_proj(x2, norm_w, w_main, w_dt, col_scale):
    m = x2.shape[0]
    return pl.pallas_call(
        _in_proj_kernel,
        grid=(m // IN_TM, PROJ_W // IN_TN),
        in_specs=[
            pl.BlockSpec((IN_TM, D_MODEL), lambda i, j: (i, 0)),
            pl.BlockSpec((1, D_MODEL), lambda i, j: (0, 0)),
            pl.BlockSpec((D_MODEL, IN_TN), lambda i, j: (0, j)),
            pl.BlockSpec((D_MODEL, DT_PAD), lambda i, j: (0, 0)),
            pl.BlockSpec((1, IN_TN), lambda i, j: (0, j)),
        ],
        out_specs=[
            pl.BlockSpec((IN_TM, IN_TN), lambda i, j: (i, j)),
            pl.BlockSpec((IN_TM, DT_PAD), lambda i, j: (i, 0)),
        ],
        out_shape=[
            jax.ShapeDtypeStruct((m, PROJ_W), BF16),
            jax.ShapeDtypeStruct((m, DT_PAD), F32),
        ],
        scratch_shapes=[pltpu.VMEM((IN_TM, D_MODEL), BF16)],
        compiler_params=pltpu.CompilerParams(
            dimension_semantics=("parallel", "arbitrary"),
            vmem_limit_bytes=VMEM_LIMIT),
        name="in_proj",
    )(x2, norm_w, w_main, w_dt, col_scale)


def _split_bf16(x, terms):
    parts = []
    r = x
    for _ in range(terms):
        p = r.astype(BF16)
        parts.append(p)
        r = r - p.astype(F32)
    return parts


def _ssd_kernel(xs_ref, bc_ref, dt_ref, z_ref, cwx_ref, cwb_ref, cbx_ref, cbb_ref,
                dtb_ref, alog_ref, dsk_ref, nw_ref, y_ref, extx, extb, state):
    c = pl.program_id(1)
    L = CHUNK
    H0 = SUBLANES

    @pl.when(c == 0)
    def _():
        extx[0:H0, :] = jnp.zeros((H0, D_SSM), F32)
        extb[0:H0, :] = jnp.zeros((H0, BC_W), F32)
        state[...] = jnp.zeros(state.shape, F32)

    extx[H0:H0 + L, :] = xs_ref[...].astype(F32)
    extb[H0:H0 + L, :] = bc_ref[...].astype(F32)

    def conv(ext, w_ref, b_ref):
        acc = b_ref[...] + w_ref[CONV_WIDTH - 1:CONV_WIDTH, :] * ext[H0:H0 + L, :]
        for w in range(CONV_WIDTH - 1):
            lo = H0 - (CONV_WIDTH - 1) + w
            acc = acc + w_ref[w:w + 1, :] * ext[lo:lo + L, :]
        return _silu(acc)

    xs = conv(extx, cwx_ref, cbx_ref)
    bc = conv(extb, cwb_ref, cbb_ref)
    extx[0:H0, :] = extx[L:L + H0, :]
    extb[0:H0, :] = extb[L:L + H0, :]

    dtr = dt_ref[...] + dtb_ref[...]
    dt = jnp.maximum(dtr, 0.0) + jnp.log1p(jnp.exp(-jnp.abs(dtr)))
    a = dt * (-jnp.exp(alog_ref[...]))

    row = lax.broadcasted_iota(jnp.int32, (L, L), 0)
    col = lax.broadcasted_iota(jnp.int32, (L, L), 1)
    tril = row >= col
    tri = jnp.where(tril, 1.0, 0.0).astype(BF16)
    a3 = jnp.concatenate(_split_bf16(a, 3), axis=1)
    cs3 = jnp.dot(tri, a3, preferred_element_type=F32)
    cs = (cs3[:, 0:LANES] + cs3[:, LANES:2 * LANES]) + cs3[:, 2 * LANES:3 * LANES]
    cs_t = cs.T

    er = lax.broadcasted_iota(jnp.int32, (LANES, D_SSM), 0)
    ec = lax.broadcasted_iota(jnp.int32, (LANES, D_SSM), 1)
    expand = jnp.where(ec // SSM_HEAD_DIM == er, 1.0, 0.0).astype(BF16)
    stack = jnp.concatenate(_split_bf16(dt, 2) + _split_bf16(cs, 2), axis=0)
    ex = jnp.dot(stack, expand, preferred_element_type=F32)
    dt_x = ex[0:L] + ex[L:2 * L]
    cs_x = ex[2 * L:3 * L] + ex[3 * L:4 * L]
    cs_last = cs_x[L - 1:L, :]

    xdt = xs * dt_x
    xdt_b = xdt.astype(BF16)
    xdec_b = (xdt * jnp.exp(cs_last - cs_x)).astype(BF16)
    dec_out = jnp.exp(cs_x)
    dec_chunk = jnp.exp(cs_last)

    lane = lax.broadcasted_iota(jnp.int32, (L, LANES), 1)
    low_half = lane < SSM_HEAD_DIM
    neg_inf = jnp.float32(-jnp.inf)

    y_parts = []
    for g in range(SSM_GROUPS):
        gsl = slice(g * GROUP_W, (g + 1) * GROUP_W)
        b_g = bc[:, g * SSM_STATE:(g + 1) * SSM_STATE].astype(BF16)
        c_g = bc[:, BC_W // 2 + g * SSM_STATE:BC_W // 2 + (g + 1) * SSM_STATE].astype(BF16)
        cb = lax.dot_general(c_g, b_g, (((1,), (1,)), ((), ())),
                             preferred_element_type=F32)
        s_in = state[g]
        y_off = jnp.dot(c_g, s_in.astype(BF16), preferred_element_type=F32) * dec_out[:, gsl]
        s_new = lax.dot_general(b_g, xdec_b[:, gsl], (((0,), (0,)), ((), ())),
                                preferred_element_type=F32)
        state[g] = s_in * dec_chunk[:, gsl] + s_new

        pair_out = []
        for pr in range(SSM_HPG // 2):
            ms = []
            for r in (2 * pr, 2 * pr + 1):
                h = g * SSM_HPG + r
                diff = cs[:, h:h + 1] - cs_t[h:h + 1, :]
                decay = jnp.exp(jnp.where(tril, diff, neg_inf))
                ms.append((cb * decay).astype(BF16))
            m2 = jnp.concatenate(ms, axis=1)
            c0 = g * GROUP_W + pr * LANES
            xp = xdt_b[:, c0:c0 + LANES]
            zero = jnp.zeros_like(xp)
            rhs = jnp.concatenate([jnp.where(low_half, xp, zero),
                                   jnp.where(low_half, zero, xp)], axis=0)
            pair_out.append(jnp.dot(m2, rhs, preferred_element_type=F32))
        y_parts.append(jnp.concatenate(pair_out, axis=1) + y_off)
    y = jnp.concatenate(y_parts, axis=1)

    y = y + dsk_ref[...] * xs
    y = y * _silu(z_ref[...].astype(F32))
    outs = []
    for g in range(SSM_GROUPS):
        gsl = slice(g * GROUP_W, (g + 1) * GROUP_W)
        yg = y[:, gsl]
        ms = jnp.mean(yg * yg, axis=-1, keepdims=True)
        outs.append((yg * lax.rsqrt(ms + NORM_EPS)) * nw_ref[:, gsl])
    y_ref[...] = jnp.concatenate(outs, axis=1).astype(y_ref.dtype)


def _ssd(proj3, dt3, cwx, cwb, cbx, cbb, dtb, alog, dsk, nw):
    b, s, _ = proj3.shape
    nc = s // CHUNK
    full = lambda shape: pl.BlockSpec(shape, lambda bi, ci: (0, 0))
    return pl.pallas_call(
        _ssd_kernel,
        grid=(b, nc),
        in_specs=[
            pl.BlockSpec((None, CHUNK, D_SSM), lambda bi, ci: (bi, ci, OFF_XS // D_SSM)),
            pl.BlockSpec((None, CHUNK, BC_W), lambda bi, ci: (bi, ci, OFF_BC // BC_W)),
            pl.BlockSpec((None, CHUNK, DT_PAD), lambda bi, ci: (bi, ci, 0)),
            pl.BlockSpec((None, CHUNK, D_SSM), lambda bi, ci: (bi, ci, OFF_ZS // D_SSM)),
            full((CONV_WIDTH, D_SSM)), full((CONV_WIDTH, BC_W)),
            full((1, D_SSM)), full((1, BC_W)),
            full((1, DT_PAD)), full((1, DT_PAD)),
            full((1, D_SSM)), full((1, D_SSM)),
        ],
        out_specs=pl.BlockSpec((None, CHUNK, D_SSM), lambda bi, ci: (bi, ci, 0)),
        out_shape=jax.ShapeDtypeStruct((b, s, D_SSM), BF16),
        scratch_shapes=[
            pltpu.VMEM((CHUNK + 2 * SUBLANES, D_SSM), F32),
            pltpu.VMEM((CHUNK + 2 * SUBLANES, BC_W), F32),
            pltpu.VMEM((SSM_GROUPS, SSM_STATE, GROUP_W), F32),
        ],
        compiler_params=pltpu.CompilerParams(
            dimension_semantics=("parallel", "arbitrary"),
            vmem_limit_bytes=VMEM_LIMIT),
        name="ssd",
    )(proj3, proj3, dt3, proj3, cwx, cwb, cbx, cbb, dtb, alog, dsk, nw)


def _attn_kernel(lam_init, q_ref, k_ref, v_ref, z_ref, lamv_ref, sw_ref, o_ref,
                 vt_scr, bias_scr, s_scr, p_scr, acc_scr):
    i = pl.program_id(2)
    tq = tk = AT_T
    nkb = vt_scr.shape[1]
    heads = range(AT_HPS)
    streams = range(2 * AT_HPS)
    hcols = lambda hh: slice(hh * ATTN_VDIM, (hh + 1) * ATTN_VDIM)

    @pl.when(i == 0)
    def _():
        kpos = lax.broadcasted_iota(jnp.int32, (tk, tq), 0)
        qpos = lax.broadcasted_iota(jnp.int32, (tk, tq), 1)
        bias_scr[...] = jnp.where(kpos <= qpos, 0.0, -jnp.inf).astype(F32)
        ones = jnp.ones((AT_ACC_ROWS - ATTN_VDIM, tk), BF16)
        for hh in heads:
            for cblk in range(nkb):
                vb = v_ref[cblk * tk:(cblk + 1) * tk, hcols(hh)].astype(F32)
                vt_scr[hh, cblk, 0:ATTN_VDIM, :] = vb.T.astype(BF16)
                vt_scr[hh, cblk, ATTN_VDIM:AT_ACC_ROWS, :] = ones

    sub = lax.broadcasted_iota(jnp.int32, (ATTN_VDIM, tq), 0)
    q_t = []
    for hh in heads:
        qh_t = q_ref[:, hcols(hh)].astype(F32).T
        q_t.append(jnp.where(sub < ATTN_HEAD_DIM, qh_t, 0.0).astype(BF16))
        q_t.append(jnp.where(sub >= ATTN_HEAD_DIM, qh_t, 0.0).astype(BF16))

    acc_scr[...] = jnp.zeros(acc_scr.shape, F32)
    p_scr[...] = jnp.zeros(p_scr.shape, BF16)
    neg_inf = jnp.float32(-jnp.inf)

    def scores(j, n):
        start = pl.multiple_of(j * tk, tk)
        k_b = k_ref[pl.ds(start, tk), hcols(n // 2)]
        s_scr[n] = jnp.dot(k_b, q_t[n], preferred_element_type=F32)

    def softmax(m, n, masked):
        s = s_scr[n]
        if masked:
            s = s + bias_scr[...]
        m_new = jnp.maximum(m, jnp.max(s, axis=0, keepdims=True))
        p_scr[n] = jnp.exp2(s - m_new).astype(BF16)
        return m_new, jnp.exp2(m - m_new)

    def accumulate(j, n, alpha):
        vt_b = vt_scr[n // 2, jnp.maximum(j, 0)]
        acc_scr[n] = alpha * acc_scr[n] + jnp.dot(
            vt_b, p_scr[n], preferred_element_type=F32)

    def step(j, carry, masked=False, prefetch=True):
        ms, alphas = carry
        new = []
        for n in streams:
            accumulate(j - 1, n, alphas[n])
            new.append(softmax(ms[n], n, masked))
            if prefetch:
                scores(j + 1, n)
        return tuple(x[0] for x in new), tuple(x[1] for x in new)

    row = lambda v: jnp.full((1, tq), v, F32)
    carry = (tuple(row(neg_inf) for _ in streams), tuple(row(1.0) for _ in streams))
    for n in streams:
        scores(0, n)
    carry = lax.fori_loop(0, i, lambda j, cr: step(j, cr), carry)
    ms, alphas = carry
    for n in streams:
        accumulate(i - 1, n, alphas[n])
    alphas = [softmax(ms[n], n, True)[1] for n in streams]
    for n in streams:
        accumulate(i, n, alphas[n])

    lv = lamv_ref[...]
    lam = (jnp.exp(jnp.sum(lv[0:1] * lv[1:2], axis=-1, keepdims=True))
           - jnp.exp(jnp.sum(lv[2:3] * lv[3:4], axis=-1, keepdims=True))
           + lam_init)
    for hh in heads:
        n1, n2 = 2 * hh, 2 * hh + 1
        l1 = acc_scr[n1, ATTN_VDIM:ATTN_VDIM + 1, :]
        l2 = acc_scr[n2, ATTN_VDIM:ATTN_VDIM + 1, :]
        o_t = (acc_scr[n1, 0:ATTN_VDIM, :] * (1.0 / l1)
               - lam * (acc_scr[n2, 0:ATTN_VDIM, :] * (1.0 / l2)))
        o = o_t.T
        ms = jnp.mean(o * o, axis=-1, keepdims=True)
        on = (o * lax.rsqrt(ms + NORM_EPS)) * sw_ref[...] * (1.0 - lam_init)
        gate = _silu(z_ref[:, hcols(hh)].astype(F32))
        o_ref[:, hcols(hh)] = (on * gate).astype(o_ref.dtype)


def _diff_attn(proj3, lamv, subln_w, lam_init):
    b, s, _ = proj3.shape
    nq = s // AT_T
    gw = AT_HPS * ATTN_VDIM
    gb = lambda off: off // gw
    nstream = 2 * AT_HPS
    return pl.pallas_call(
        functools.partial(_attn_kernel, lam_init),
        grid=(b, ATTN_HEADS // AT_HPS, nq),
        in_specs=[
            pl.BlockSpec((None, AT_T, gw), lambda bi, g, i: (bi, i, gb(OFF_Q) + g)),
            pl.BlockSpec((None, s, gw), lambda bi, g, i: (bi, 0, gb(OFF_K) + g)),
            pl.BlockSpec((None, s, gw), lambda bi, g, i: (bi, 0, gb(OFF_V) + g)),
            pl.BlockSpec((None, AT_T, gw), lambda bi, g, i: (bi, i, gb(OFF_ZA) + g)),
            pl.BlockSpec((4, ATTN_HEAD_DIM), lambda bi, g, i: (0, 0)),
            pl.BlockSpec((1, ATTN_VDIM), lambda bi, g, i: (0, 0)),
        ],
        out_specs=pl.BlockSpec((None, AT_T, gw), lambda bi, g, i: (bi, i, g)),
        out_shape=jax.ShapeDtypeStruct((b, s, D_ATTN), BF16),
        scratch_shapes=[
            pltpu.VMEM((AT_HPS, s // AT_T, AT_ACC_ROWS, AT_T), BF16),
            pltpu.VMEM((AT_T, AT_T), F32),
            pltpu.VMEM((nstream, AT_T, AT_T), F32),
            pltpu.VMEM((nstream, AT_T, AT_T), BF16),
            pltpu.VMEM((nstream, AT_ACC_ROWS, AT_T), F32),
        ],
        compiler_params=pltpu.CompilerParams(
            dimension_semantics=("parallel", "parallel", "arbitrary"),
            vmem_limit_bytes=VMEM_LIMIT),
        name="diffattn",
    )(proj3, proj3, proj3, proj3, lamv, subln_w)


def _out_proj_kernel(ys_ref, ya_ref, ws_ref, wa_ref, x_ref, nw_ref, o_ref):
    ys = ys_ref[...]
    ya = ya_ref[...]
    ssq = jnp.zeros((OUT_TM, 1), F32)
    for jj in range(D_MODEL // OUT_TN):
        sl = slice(jj * OUT_TN, (jj + 1) * OUT_TN)
        acc = (jnp.dot(ys, ws_ref[:, sl], preferred_element_type=F32)
               + jnp.dot(ya, wa_ref[:, sl], preferred_element_type=F32))
        ssq = ssq + jnp.sum(acc * acc, axis=-1, keepdims=True)
        o_ref[:, sl] = acc
    inv = lax.rsqrt(ssq * (1.0 / D_MODEL) + NORM_EPS)
    for r in range(OUT_TM // OUT_RC):
        rows = slice(r * OUT_RC, (r + 1) * OUT_RC)
        inv_r = inv[rows]
        for jj in range(D_MODEL // OUT_TN):
            sl = slice(jj * OUT_TN, (jj + 1) * OUT_TN)
            o_ref[rows, sl] = x_ref[rows, sl] + (o_ref[rows, sl] * inv_r) * nw_ref[:, sl]


def _out_proj(ys, ya, w_s, w_a, x2, norm_w):
    m = x2.shape[0]
    resident = lambda rows: pl.BlockSpec((rows, D_MODEL), lambda i: (0, 0),
                                         pipeline_mode=pl.Buffered(1))
    return pl.pallas_call(
        _out_proj_kernel,
        grid=(m // OUT_TM,),
        in_specs=[
            pl.BlockSpec((OUT_TM, D_SSM), lambda i: (i, 0)),
            pl.BlockSpec((OUT_TM, D_ATTN), lambda i: (i, 0)),
            resident(D_SSM),
            resident(D_ATTN),
            pl.BlockSpec((OUT_TM, D_MODEL), lambda i: (i, 0)),
            pl.BlockSpec((1, D_MODEL), lambda i: (0, 0)),
        ],
        out_specs=pl.BlockSpec((OUT_TM, D_MODEL), lambda i: (i, 0)),
        out_shape=jax.ShapeDtypeStruct((m, D_MODEL), F32),
        compiler_params=pltpu.CompilerParams(
            dimension_semantics=("parallel",),
            vmem_limit_bytes=OUT_VMEM_LIMIT),
        name="out_proj",
    )(ys, ya, w_s, w_a, x2, norm_w)


def _layer(x, layer, pre_norm_w, post_norm_w, w_in, conv_w, conv_b, dt_bias, a_log, d_skip,
           ssm_norm_w, lambda_q1, lambda_k1, lambda_q2, lambda_k2, attn_subln_w, w_out):
    b, s, d = x.shape
    x2 = x.reshape(b * s, d)

    w_main = jnp.concatenate([w_in[:, :W_IN_DT], w_in[:, W_IN_QKV:]], axis=1).astype(BF16)
    w_dt = jnp.pad(w_in[:, W_IN_DT:W_IN_QKV], ((0, 0), (0, DT_PAD - SSM_HEADS))).astype(BF16)

    q_scale = ATTN_HEAD_DIM ** -0.5 * math.log2(math.e)
    col_scale = jnp.ones((1, PROJ_W), F32).at[:, OFF_Q:OFF_Q + D_ATTN].set(q_scale)
    proj, dt_raw = _in_proj(x2, pre_norm_w.reshape(1, d), w_main, w_dt, col_scale)
    proj3 = proj.reshape(b, s, PROJ_W)
    dt3 = dt_raw.reshape(b, s, DT_PAD)

    pad_h = lambda v: jnp.pad(v.reshape(1, SSM_HEADS), ((0, 0), (0, DT_PAD - SSM_HEADS)))
    y_ssm = _ssd(
        proj3, dt3,
        conv_w[:, :D_SSM], conv_w[:, D_SSM:],
        conv_b[:D_SSM].reshape(1, D_SSM), conv_b[D_SSM:].reshape(1, BC_W),
        pad_h(dt_bias), pad_h(a_log),
        jnp.repeat(d_skip, SSM_HEAD_DIM).reshape(1, D_SSM),
        ssm_norm_w.reshape(1, D_SSM))

    lam_init = 0.8 - 0.6 * math.exp(-0.3 * layer)
    lamv = jnp.stack([lambda_q1, lambda_k1, lambda_q2, lambda_k2], axis=0)
    y_attn = _diff_attn(proj3, lamv, attn_subln_w.reshape(1, ATTN_VDIM), lam_init)

    w_o = w_out.astype(BF16)
    out = _out_proj(y_ssm.reshape(b * s, D_SSM), y_attn.reshape(b * s, D_ATTN),
                    w_o[:D_SSM], w_o[D_SSM:], x2, post_norm_w.reshape(1, d))
    return out.reshape(b, s, d)


def kernel(x, pre_norm_w, post_norm_w, w_in, conv_w, conv_b, dt_bias, a_log, d_skip, ssm_norm_w, lambda_q1, lambda_k1, lambda_q2, lambda_k2, attn_subln_w, w_out):
    for layer in range(w_in.shape[0]):
        x = _layer(x, layer, pre_norm_w[layer], post_norm_w[layer], w_in[layer],
                   conv_w[layer], conv_b[layer], dt_bias[layer], a_log[layer], d_skip[layer],
                   ssm_norm_w[layer], lambda_q1[layer], lambda_k1[layer], lambda_q2[layer],
                   lambda_k2[layer], attn_subln_w[layer], w_out[layer])
    return x
```

```python
import functools
import math

import jax
import jax.numpy as jnp
from jax import lax
from jax.experimental import pallas as pl
from jax.experimental.pallas import tpu as pltpu

F32 = jnp.float32
BF16 = jnp.bfloat16

D_MODEL = 4096
D_SSM = 2048
D_ATTN = 2048
SSM_HEAD_DIM = 64
SSM_HEADS = 32
SSM_GROUPS = 4
SSM_HPG = 8
SSM_STATE = 128
CONV_WIDTH = 4
CHUNK = 128
ATTN_HEAD_DIM = 64
ATTN_VDIM = 128
ATTN_HEADS = 16
NORM_EPS = 1e-6
GROUP_W = D_SSM // SSM_GROUPS
BC_W = 2 * SSM_GROUPS * SSM_STATE

LANES = 128
SUBLANES = 8
VMEM_LIMIT = 56 * 1024 * 1024

OFF_ZS = 0
OFF_ZA = 2048
OFF_XS = 4096
OFF_BC = 6144
OFF_Q = 7168
OFF_K = 9216
OFF_V = 11264
PROJ_W = 13312
W_IN_DT = OFF_Q
W_IN_QKV = W_IN_DT + SSM_HEADS
DT_PAD = LANES

IN_TM, IN_TN = 512, 1024
AT_T = 512
AT_DW = 256
AT_HPS = 4
BF16_SUBLANES = 16
AT_ACC_ROWS = ATTN_VDIM + BF16_SUBLANES
OUT_TM, OUT_TN, OUT_RC = 256, 512, 64
OUT_VMEM_LIMIT = 60 * 1024 * 1024


def _silu(x):
    hx = 0.5 * x
    return hx + hx * jnp.tanh(hx)


def _in_proj_kernel(x_ref, nw_ref, w_ref, wdt_ref, cs_ref, o_ref, dt_ref, h_scr):
    j = pl.program_id(1)

    @pl.when(j == 0)
    def _():
        x = x_ref[...]
        ms = jnp.mean(x * x, axis=-1, keepdims=True)
        h = (x * lax.rsqrt(ms + NORM_EPS)) * nw_ref[...]
        hb = h.astype(BF16)
        h_scr[...] = hb
        dt_ref[...] = jnp.dot(hb, wdt_ref[...], preferred_element_type=F32)

    acc = jnp.dot(h_scr[...], w_ref[...], preferred_element_type=F32)
    o_ref[...] = (acc * cs_ref[...]).astype(o_ref.dtype)


def _in_proj(x2, norm_w, w_main, w_dt, col_scale):
    m = x2.shape[0]
    return pl.pallas_call(
        _in_proj_kernel,
        grid=(m // IN_TM, PROJ_W // IN_TN),
        in_specs=[
            pl.BlockSpec((IN_TM, D_MODEL), lambda i, j: (i, 0)),
            pl.BlockSpec((1, D_MODEL), lambda i, j: (0, 0)),
            pl.BlockSpec((D_MODEL, IN_TN), lambda i, j: (0, j)),
            pl.BlockSpec((D_MODEL, DT_PAD), lambda i, j: (0, 0)),
            pl.BlockSpec((1, IN_TN), lambda i, j: (0, j)),
        ],
        out_specs=[
            pl.BlockSpec((IN_TM, IN_TN), lambda i, j: (i, j)),
            pl.BlockSpec((IN_TM, DT_PAD), lambda i, j: (i, 0)),
        ],
        out_shape=[
            jax.ShapeDtypeStruct((m, PROJ_W), BF16),
            jax.ShapeDtypeStruct((m, DT_PAD), F32),
        ],
        scratch_shapes=[pltpu.VMEM((IN_TM, D_MODEL), BF16)],
        compiler_params=pltpu.CompilerParams(
            dimension_semantics=("parallel", "arbitrary"),
            vmem_limit_bytes=VMEM_LIMIT),
        name="in_proj",
    )(x2, norm_w, w_main, w_dt, col_scale)


def _split_bf16(x, terms):
    parts = []
    r = x
    for _ in range(terms):
        p = r.astype(BF16)
        parts.append(p)
        r = r - p.astype(F32)
    return parts


def _ssd_kernel(xs_ref, bc_ref, dt_ref, z_ref, cwx_ref, cwb_ref, cbx_ref, cbb_ref,
                dtb_ref, alog_ref, dsk_ref, nw_ref, y_ref, extx, extb, shift, state):
    c = pl.program_id(1)
    L = CHUNK
    H0 = BF16_SUBLANES

    @pl.when(c == 0)
    def _():
        extx[0:H0, :] = jnp.zeros((H0, D_SSM), BF16)
        extb[0:H0, :] = jnp.zeros((H0, BC_W), BF16)
        state[...] = jnp.zeros(state.shape, F32)
        out_row = lax.broadcasted_iota(jnp.int32, shift.shape, 0)
        src_row = lax.broadcasted_iota(jnp.int32, shift.shape, 1)
        want = H0 + out_row % L - (CONV_WIDTH - 1) + out_row // L
        shift[...] = jnp.where(src_row == want, 1.0, 0.0).astype(BF16)

    extx[H0:H0 + L, :] = xs_ref[...]
    extb[H0:H0 + L, :] = bc_ref[...]

    def conv(ext, w_ref, b_ref):
        taps = jnp.dot(shift[...], ext[...], preferred_element_type=F32)
        acc = b_ref[...] + w_ref[0:1, :] * taps[0:L]
        for w in range(1, CONV_WIDTH):
            acc = acc + w_ref[w:w + 1, :] * taps[w * L:(w + 1) * L]
        return _silu(acc)

    xs = conv(extx, cwx_ref, cbx_ref)
    bc = conv(extb, cwb_ref, cbb_ref)
    extx[0:H0, :] = extx[L:L + H0, :]
    extb[0:H0, :] = extb[L:L + H0, :]

    dtr = dt_ref[...] + dtb_ref[...]
    dt = jnp.maximum(dtr, 0.0) + jnp.log1p(jnp.exp(-jnp.abs(dtr)))
    a = dt * (-jnp.exp(alog_ref[...]))

    row = lax.broadcasted_iota(jnp.int32, (L, L), 0)
    col = lax.broadcasted_iota(jnp.int32, (L, L), 1)
    tril = row >= col
    tri = jnp.where(tril, 1.0, 0.0).astype(BF16)
    a3 = jnp.concatenate(_split_bf16(a, 3), axis=1)
    cs3 = jnp.dot(tri, a3, preferred_element_type=F32)
    cs = (cs3[:, 0:LANES] + cs3[:, LANES:2 * LANES]) + cs3[:, 2 * LANES:3 * LANES]
    cs_t = cs.T

    er = lax.broadcasted_iota(jnp.int32, (LANES, D_SSM), 0)
    ec = lax.broadcasted_iota(jnp.int32, (LANES, D_SSM), 1)
    expand = jnp.where(ec // SSM_HEAD_DIM == er, 1.0, 0.0).astype(BF16)
    stack = jnp.concatenate(_split_bf16(dt, 2) + _split_bf16(cs, 2), axis=0)
    ex = jnp.dot(stack, expand, preferred_element_type=F32)
    dt_x = ex[0:L] + ex[L:2 * L]
    cs_x = ex[2 * L:3 * L] + ex[3 * L:4 * L]
    cs_last = cs_x[L - 1:L, :]

    xdt = xs * dt_x
    xdt_b = xdt.astype(BF16)
    xdec_b = (xdt * jnp.exp(cs_last - cs_x)).astype(BF16)
    dec_out = jnp.exp(cs_x)
    dec_chunk = jnp.exp(cs_last)

    lane = lax.broadcasted_iota(jnp.int32, (L, LANES), 1)
    low_half = lane < SSM_HEAD_DIM
    neg_inf = jnp.float32(-jnp.inf)

    y_parts = []
    for g in range(SSM_GROUPS):
        gsl = slice(g * GROUP_W, (g + 1) * GROUP_W)
        b_g = bc[:, g * SSM_STATE:(g + 1) * SSM_STATE].astype(BF16)
        c_g = bc[:, BC_W // 2 + g * SSM_STATE:BC_W // 2 + (g + 1) * SSM_STATE].astype(BF16)
        cb = lax.dot_general(c_g, b_g, (((1,), (1,)), ((), ())),
                             preferred_element_type=F32)
        s_in = state[g]
        y_off = jnp.dot(c_g, s_in.astype(BF16), preferred_element_type=F32) * dec_out[:, gsl]
        s_new = lax.dot_general(b_g, xdec_b[:, gsl], (((0,), (0,)), ((), ())),
                                preferred_element_type=F32)
        state[g] = s_in * dec_chunk[:, gsl] + s_new

        pair_out = []
        for pr in range(SSM_HPG // 2):
            ms = []
            for r in (2 * pr, 2 * pr + 1):
                h = g * SSM_HPG + r
                diff = cs[:, h:h + 1] - cs_t[h:h + 1, :]
                decay = jnp.exp(jnp.where(tril, diff, neg_inf))
                ms.append((cb * decay).astype(BF16))
            m2 = jnp.concatenate(ms, axis=1)
            c0 = g * GROUP_W + pr * LANES
            xp = xdt_b[:, c0:c0 + LANES]
            zero = jnp.zeros_like(xp)
            rhs = jnp.concatenate([jnp.where(low_half, xp, zero),
                                   jnp.where(low_half, zero, xp)], axis=0)
            pair_out.append(jnp.dot(m2, rhs, preferred_element_type=F32))
        y_parts.append(jnp.concatenate(pair_out, axis=1) + y_off)
    y = jnp.concatenate(y_parts, axis=1)

    y = y + dsk_ref[...] * xs
    y = y * _silu(z_ref[...].astype(F32))
    outs = []
    for g in range(SSM_GROUPS):
        gsl = slice(g * GROUP_W, (g + 1) * GROUP_W)
        yg = y[:, gsl]
        ms = jnp.mean(yg * yg, axis=-1, keepdims=True)
        outs.append((yg * lax.rsqrt(ms + NORM_EPS)) * nw_ref[:, gsl])
    y_ref[...] = jnp.concatenate(outs, axis=1).astype(y_ref.dtype)


def _ssd(proj3, dt3, cwx, cwb, cbx, cbb, dtb, alog, dsk, nw):
    b, s, _ = proj3.shape
    nc = s // CHUNK
    full = lambda shape: pl.BlockSpec(shape, lambda bi, ci: (0, 0))
    return pl.pallas_call(
        _ssd_kernel,
        grid=(b, nc),
        in_specs=[
            pl.BlockSpec((None, CHUNK, D_SSM), lambda bi, ci: (bi, ci, OFF_XS // D_SSM)),
            pl.BlockSpec((None, CHUNK, BC_W), lambda bi, ci: (bi, ci, OFF_BC // BC_W)),
            pl.BlockSpec((None, CHUNK, DT_PAD), lambda bi, ci: (bi, ci, 0)),
            pl.BlockSpec((None, CHUNK, D_SSM), lambda bi, ci: (bi, ci, OFF_ZS // D_SSM)),
            full((CONV_WIDTH, D_SSM)), full((CONV_WIDTH, BC_W)),
            full((1, D_SSM)), full((1, BC_W)),
            full((1, DT_PAD)), full((1, DT_PAD)),
            full((1, D_SSM)), full((1, D_SSM)),
        ],
        out_specs=pl.BlockSpec((None, CHUNK, D_SSM), lambda bi, ci: (bi, ci, 0)),
        out_shape=jax.ShapeDtypeStruct((b, s, D_SSM), BF16),
        scratch_shapes=[
            pltpu.VMEM((BF16_SUBLANES + CHUNK, D_SSM), BF16),
            pltpu.VMEM((BF16_SUBLANES + CHUNK, BC_W), BF16),
            pltpu.VMEM((CONV_WIDTH * CHUNK, BF16_SUBLANES + CHUNK), BF16),
            pltpu.VMEM((SSM_GROUPS, SSM_STATE, GROUP_W), F32),
        ],
        compiler_params=pltpu.CompilerParams(
            dimension_semantics=("parallel", "arbitrary"),
            vmem_limit_bytes=VMEM_LIMIT),
        name="ssd",
    )(proj3, proj3, dt3, proj3, cwx, cwb, cbx, cbb, dtb, alog, dsk, nw)


def _attn_kernel(lam_init, q_ref, k_ref, v_ref, z_ref, lamv_ref, sw_ref, o_ref,
                 vt_scr, bias_scr, s_scr, p_scr, acc_scr):
    i = pl.program_id(2)
    tq = tk = AT_T
    nkb = vt_scr.shape[1]
    heads = range(AT_HPS)
    streams = range(2 * AT_HPS)
    hcols = lambda hh: slice(hh * ATTN_VDIM, (hh + 1) * ATTN_VDIM)

    @pl.when(i == 0)
    def _():
        kpos = lax.broadcasted_iota(jnp.int32, (tk, tq), 0)
        qpos = lax.broadcasted_iota(jnp.int32, (tk, tq), 1)
        bias_scr[...] = jnp.where(kpos <= qpos, 0.0, -jnp.inf).astype(F32)
        ones = jnp.ones((AT_ACC_ROWS - ATTN_VDIM, tk), BF16)
        for hh in heads:
            for cblk in range(nkb):
                vb = v_ref[cblk * tk:(cblk + 1) * tk, hcols(hh)].astype(F32)
                vt_scr[hh, cblk, 0:ATTN_VDIM, :] = vb.T.astype(BF16)
                vt_scr[hh, cblk, ATTN_VDIM:AT_ACC_ROWS, :] = ones

    sub = lax.broadcasted_iota(jnp.int32, (ATTN_VDIM, tq), 0)
    q_t = []
    for hh in heads:
        qh_t = q_ref[:, hcols(hh)].astype(F32).T
        q_t.append(jnp.where(sub < ATTN_HEAD_DIM, qh_t, 0.0).astype(BF16))
        q_t.append(jnp.where(sub >= ATTN_HEAD_DIM, qh_t, 0.0).astype(BF16))

    acc_scr[...] = jnp.zeros(acc_scr.shape, F32)
    p_scr[...] = jnp.zeros(p_scr.shape, BF16)
    neg_inf = jnp.float32(-jnp.inf)

    def scores(j, n):
        start = pl.multiple_of(j * tk, tk)
        k_b = k_ref[pl.ds(start, tk), hcols(n // 2)]
        s_scr[n] = jnp.dot(k_b, q_t[n], preferred_element_type=F32)

    def softmax(m, n):
        s = s_scr[n]
        m_new = jnp.maximum(m, jnp.max(s, axis=0, keepdims=True))
        p_scr[n] = jnp.exp2(s - m_new).astype(BF16)
        return m_new, jnp.exp2(m - m_new)

    def accumulate(j, n, alpha):
        vt_b = vt_scr[n // 2, jnp.maximum(j, 0)]
        acc_scr[n] = alpha * acc_scr[n] + jnp.dot(
            vt_b, p_scr[n], preferred_element_type=F32)

    def step(j, carry):
        ms, alphas = carry
        new = []
        for n in streams:
            accumulate(j - 1, n, alphas[n])
            new.append(softmax(ms[n], n))
            scores(j + 1, n)
        return tuple(x[0] for x in new), tuple(x[1] for x in new)

    row = lambda v: jnp.full((1, tq), v, F32)
    carry = (tuple(row(neg_inf) for _ in streams), tuple(row(1.0) for _ in streams))
    for n in streams:
        scores(0, n)
    carry = lax.fori_loop(0, i, lambda j, cr: step(j, cr), carry)
    def diag_tiles():
        for c0 in range(0, tq, AT_DW):
            yield slice(0, c0 + AT_DW), slice(c0, c0 + AT_DW)

    def softmax_diag(m, n):
        alpha = []
        for rows, cols in diag_tiles():
            s = s_scr[n, rows, cols] + bias_scr[rows, cols]
            m_old = m[:, cols]
            m_new = jnp.maximum(m_old, jnp.max(s, axis=0, keepdims=True))
            p_scr[n, rows, cols] = jnp.exp2(s - m_new).astype(BF16)
            alpha.append(jnp.exp2(m_old - m_new))
        return jnp.concatenate(alpha, axis=1)

    def accumulate_diag(n, alpha):
        for rows, cols in diag_tiles():
            acc_scr[n, :, cols] = alpha[:, cols] * acc_scr[n, :, cols] + jnp.dot(
                vt_scr[n // 2, i, :, rows], p_scr[n, rows, cols], preferred_element_type=F32)

    ms, alphas = carry
    for n in streams:
        accumulate(i - 1, n, alphas[n])
        accumulate_diag(n, softmax_diag(ms[n], n))

    lv = lamv_ref[...]
    lam = (jnp.exp(jnp.sum(lv[0:1] * lv[1:2], axis=-1, keepdims=True))
           - jnp.exp(jnp.sum(lv[2:3] * lv[3:4], axis=-1, keepdims=True))
           + lam_init)
    for hh in heads:
        n1, n2 = 2 * hh, 2 * hh + 1
        l1 = acc_scr[n1, ATTN_VDIM:ATTN_VDIM + 1, :]
        l2 = acc_scr[n2, ATTN_VDIM:ATTN_VDIM + 1, :]
        o_t = (acc_scr[n1, 0:ATTN_VDIM, :] * (1.0 / l1)
               - lam * (acc_scr[n2, 0:ATTN_VDIM, :] * (1.0 / l2)))
        o = o_t.T
        ms = jnp.mean(o * o, axis=-1, keepdims=True)
        on = (o * lax.rsqrt(ms + NORM_EPS)) * sw_ref[...] * (1.0 - lam_init)
        gate = _silu(z_ref[:, hcols(hh)].astype(F32))
        o_ref[:, hcols(hh)] = (on * gate).astype(o_ref.dtype)


def _diff_attn(proj3, lamv, subln_w, lam_init):
    b, s, _ = proj3.shape
    nq = s // AT_T
    gw = AT_HPS * ATTN_VDIM
    gb = lambda off: off // gw
    nstream = 2 * AT_HPS
    return pl.pallas_call(
        functools.partial(_attn_kernel, lam_init),
        grid=(b, ATTN_HEADS // AT_HPS, nq),
        in_specs=[
            pl.BlockSpec((None, AT_T, gw), lambda bi, g, i: (bi, i, gb(OFF_Q) + g)),
            pl.BlockSpec((None, s, gw), lambda bi, g, i: (bi, 0, gb(OFF_K) + g)),
            pl.BlockSpec((None, s, gw), lambda bi, g, i: (bi, 0, gb(OFF_V) + g)),
            pl.BlockSpec((None, AT_T, gw), lambda bi, g, i: (bi, i, gb(OFF_ZA) + g)),
            pl.BlockSpec((4, ATTN_HEAD_DIM), lambda bi, g, i: (0, 0)),
            pl.BlockSpec((1, ATTN_VDIM), lambda bi, g, i: (0, 0)),
        ],
        out_specs=pl.BlockSpec((None, AT_T, gw), lambda bi, g, i: (bi, i, g)),
        out_shape=jax.ShapeDtypeStruct((b, s, D_ATTN), BF16),
        scratch_shapes=[
            pltpu.VMEM((AT_HPS, s // AT_T, AT_ACC_ROWS, AT_T), BF16),
            pltpu.VMEM((AT_T, AT_T), F32),
            pltpu.VMEM((nstream, AT_T, AT_T), F32),
            pltpu.VMEM((nstream, AT_T, AT_T), BF16),
            pltpu.VMEM((nstream, AT_ACC_ROWS, AT_T), F32),
        ],
        compiler_params=pltpu.CompilerParams(
            dimension_semantics=("parallel", "parallel", "arbitrary"),
            vmem_limit_bytes=VMEM_LIMIT),
        name="diffattn",
    )(proj3, proj3, proj3, proj3, lamv, subln_w)


def _out_proj_kernel(ys_ref, ya_ref, w_ref, x_ref, nw_ref, o_ref):
    ys = ys_ref[...]
    ya = ya_ref[...]
    ssq = jnp.zeros((OUT_TM, 1), F32)
    for jj in range(D_MODEL // OUT_TN):
        sl = slice(jj * OUT_TN, (jj + 1) * OUT_TN)
        acc = (jnp.dot(ys, w_ref[0:D_SSM, sl], preferred_element_type=F32)
               + jnp.dot(ya, w_ref[D_SSM:D_SSM + D_ATTN, sl], preferred_element_type=F32))
        ssq = ssq + jnp.sum(acc * acc, axis=-1, keepdims=True)
        o_ref[:, sl] = acc
    inv = lax.rsqrt(ssq * (1.0 / D_MODEL) + NORM_EPS)
    for r in range(OUT_TM // OUT_RC):
        rows = slice(r * OUT_RC, (r + 1) * OUT_RC)
        inv_r = inv[rows]
        for jj in range(D_MODEL // OUT_TN):
            sl = slice(jj * OUT_TN, (jj + 1) * OUT_TN)
            o_ref[rows, sl] = x_ref[rows, sl] + (o_ref[rows, sl] * inv_r) * nw_ref[:, sl]


def _out_proj(ys, ya, w_o, x2, norm_w):
    m = x2.shape[0]
    return pl.pallas_call(
        _out_proj_kernel,
        grid=(m // OUT_TM,),
        in_specs=[
            pl.BlockSpec((OUT_TM, D_SSM), lambda i: (i, 0)),
            pl.BlockSpec((OUT_TM, D_ATTN), lambda i: (i, 0)),
            pl.BlockSpec((D_SSM + D_ATTN, D_MODEL), lambda i: (0, 0),
                         pipeline_mode=pl.Buffered(1)),
            pl.BlockSpec((OUT_TM, D_MODEL), lambda i: (i, 0)),
            pl.BlockSpec((1, D_MODEL), lambda i: (0, 0)),
        ],
        out_specs=pl.BlockSpec((OUT_TM, D_MODEL), lambda i: (i, 0)),
        out_shape=jax.ShapeDtypeStruct((m, D_MODEL), F32),
        compiler_params=pltpu.CompilerParams(
            dimension_semantics=("parallel",),
            vmem_limit_bytes=OUT_VMEM_LIMIT),
        name="out_proj",
    )(ys, ya, w_o, x2, norm_w)


def _layer(x, layer, pre_norm_w, post_norm_w, w_in, conv_w, conv_b, dt_bias, a_log, d_skip,
           ssm_norm_w, lambda_q1, lambda_k1, lambda_q2, lambda_k2, attn_subln_w, w_out):
    b, s, d = x.shape
    x2 = x.reshape(b * s, d)

    col = lax.broadcasted_iota(jnp.int32, (1, PROJ_W), 1)
    w_main = jnp.where(col < W_IN_DT, w_in[:, :PROJ_W], w_in[:, SSM_HEADS:]).astype(BF16)
    w_dt = jnp.pad(w_in[:, W_IN_DT:W_IN_QKV], ((0, 0), (0, DT_PAD - SSM_HEADS))).astype(BF16)

    q_scale = ATTN_HEAD_DIM ** -0.5 * math.log2(math.e)
    col_scale = jnp.ones((1, PROJ_W), F32).at[:, OFF_Q:OFF_Q + D_ATTN].set(q_scale)
    proj, dt_raw = _in_proj(x2, pre_norm_w.reshape(1, d), w_main, w_dt, col_scale)
    proj3 = proj.reshape(b, s, PROJ_W)
    dt3 = dt_raw.reshape(b, s, DT_PAD)

    pad_h = lambda v: jnp.pad(v.reshape(1, SSM_HEADS), ((0, 0), (0, DT_PAD - SSM_HEADS)))
    y_ssm = _ssd(
        proj3, dt3,
        conv_w[:, :D_SSM], conv_w[:, D_SSM:],
        conv_b[:D_SSM].reshape(1, D_SSM), conv_b[D_SSM:].reshape(1, BC_W),
        pad_h(dt_bias), pad_h(a_log),
        jnp.repeat(d_skip, SSM_HEAD_DIM).reshape(1, D_SSM),
        ssm_norm_w.reshape(1, D_SSM))

    lam_init = 0.8 - 0.6 * math.exp(-0.3 * layer)
    lamv = jnp.stack([lambda_q1, lambda_k1, lambda_q2, lambda_k2], axis=0)
    y_attn = _diff_attn(proj3, lamv, attn_subln_w.reshape(1, ATTN_VDIM), lam_init)

    out = _out_proj(y_ssm.reshape(b * s, D_SSM), y_attn.reshape(b * s, D_ATTN),
                    w_out.astype(BF16), x2, post_norm_w.reshape(1, d))
    return out.reshape(b, s, d)


def kernel(x, pre_norm_w, post_norm_w, w_in, conv_w, conv_b, dt_bias, a_log, d_skip, ssm_norm_w, lambda_q1, lambda_k1, lambda_q2, lambda_k2, attn_subln_w, w_out):
    for layer in range(w_in.shape[0]):
        x = _layer(x, layer, pre_norm_w[layer], post_norm_w[layer], w_in[layer],
                   conv_w[layer], conv_b[layer], dt_bias[layer], a_log[layer], d_skip[layer],
                   ssm_norm_w[layer], lambda_q1[layer], lambda_k1[layer], lambda_q2[layer],
                   lambda_k2[layer], attn_subln_w[layer], w_out[layer])
    return x
```

```python
import functools
import math

import jax
import jax.numpy as jnp
from jax import lax
from jax.experimental import pallas as pl
from jax.experimental.pallas import tpu as pltpu

F32 = jnp.float32
BF16 = jnp.bfloat16

D_MODEL = 4096
D_SSM = 2048
D_ATTN = 2048
SSM_HEAD_DIM = 64
SSM_HEADS = 32
SSM_GROUPS = 4
SSM_HPG = 8
SSM_STATE = 128
CONV_WIDTH = 4
CHUNK = 128
ATTN_HEAD_DIM = 64
ATTN_VDIM = 128
ATTN_HEADS = 16
NORM_EPS = 1e-6
GROUP_W = D_SSM // SSM_GROUPS
BC_W = 2 * SSM_GROUPS * SSM_STATE

LANES = 128
SUBLANES = 8
VMEM_LIMIT = 56 * 1024 * 1024

OFF_ZS = 0
OFF_ZA = 2048
OFF_XS = 4096
OFF_BC = 6144
OFF_Q = 7168
OFF_K = 9216
OFF_V = 11264
PROJ_W = 13312
W_IN_DT = OFF_Q
W_IN_QKV = W_IN_DT + SSM_HEADS
DT_PAD = LANES

IN_TM, IN_TN = 512, 1024
AT_T = 512
AT_DW = 256
AT_HPS = 4
BF16_SUBLANES = 16
AT_ACC_ROWS = ATTN_VDIM + BF16_SUBLANES
OUT_TM, OUT_TN, OUT_RC = 256, 512, 64
OUT_VMEM_LIMIT = 60 * 1024 * 1024


def _silu(x):
    hx = 0.5 * x
    return hx + hx * jnp.tanh(hx)


def _dot_nt(a, b_t):
    return lax.dot_general(a, b_t, (((1,), (1,)), ((), ())), preferred_element_type=F32)


def _in_proj_kernel(x_ref, nw_ref, wt_ref, wdt_ref, cs_ref, o_ref, dt_ref, h_scr):
    j = pl.program_id(1)

    @pl.when(j == 0)
    def _():
        x = x_ref[...]
        ms = jnp.mean(x * x, axis=-1, keepdims=True)
        h = (x * lax.rsqrt(ms + NORM_EPS)) * nw_ref[...]
        hb = h.astype(BF16)
        h_scr[...] = hb
        dt_ref[...] = _dot_nt(hb, wdt_ref[...])

    o_ref[...] = (_dot_nt(h_scr[...], wt_ref[...]) * cs_ref[...]).astype(o_ref.dtype)


def _in_proj(x2, norm_w, wt_main, wt_dt, col_scale):
    m = x2.shape[0]
    return pl.pallas_call(
        _in_proj_kernel,
        grid=(m // IN_TM, PROJ_W // IN_TN),
        in_specs=[
            pl.BlockSpec((IN_TM, D_MODEL), lambda i, j: (i, 0)),
            pl.BlockSpec((1, D_MODEL), lambda i, j: (0, 0)),
            pl.BlockSpec((pl.Element(IN_TN), pl.Element(D_MODEL)),
                         lambda i, j: (BF16_SUBLANES * (j * (IN_TN // BF16_SUBLANES) + jnp.where(
                             j * IN_TN >= W_IN_DT, SSM_HEADS // BF16_SUBLANES, 0)), 0)),
            pl.BlockSpec((DT_PAD, D_MODEL), lambda i, j: (0, 0)),
            pl.BlockSpec((1, IN_TN), lambda i, j: (0, j)),
        ],
        out_specs=[
            pl.BlockSpec((IN_TM, IN_TN), lambda i, j: (i, j)),
            pl.BlockSpec((IN_TM, DT_PAD), lambda i, j: (i, 0)),
        ],
        out_shape=[
            jax.ShapeDtypeStruct((m, PROJ_W), BF16),
            jax.ShapeDtypeStruct((m, DT_PAD), F32),
        ],
        scratch_shapes=[pltpu.VMEM((IN_TM, D_MODEL), BF16)],
        compiler_params=pltpu.CompilerParams(
            dimension_semantics=("parallel", "arbitrary"),
            vmem_limit_bytes=VMEM_LIMIT),
        name="in_proj",
    )(x2, norm_w, wt_main, wt_dt, col_scale)


def _split_bf16(x, terms):
    parts = []
    r = x
    for _ in range(terms):
        p = r.astype(BF16)
        parts.append(p)
        r = r - p.astype(F32)
    return parts


def _ssd_kernel(xs_ref, bc_ref, dt_ref, z_ref, cwx_ref, cwb_ref, cbx_ref, cbb_ref,
                dtb_ref, alog_ref, dsk_ref, nw_ref, y_ref, extx, extb, shift, state):
    c = pl.program_id(1)
    L = CHUNK
    H0 = BF16_SUBLANES

    @pl.when(c == 0)
    def _():
        extx[0:H0, :] = jnp.zeros((H0, D_SSM), BF16)
        extb[0:H0, :] = jnp.zeros((H0, BC_W), BF16)
        state[...] = jnp.zeros(state.shape, F32)
        out_row = lax.broadcasted_iota(jnp.int32, shift.shape, 0)
        src_row = lax.broadcasted_iota(jnp.int32, shift.shape, 1)
        want = H0 + out_row % L - (CONV_WIDTH - 1) + out_row // L
        shift[...] = jnp.where(src_row == want, 1.0, 0.0).astype(BF16)

    extx[H0:H0 + L, :] = xs_ref[...]
    extb[H0:H0 + L, :] = bc_ref[...]

    def conv(ext, w_ref, b_ref):
        taps = jnp.dot(shift[...], ext[...], preferred_element_type=F32)
        acc = b_ref[...] + w_ref[0:1, :] * taps[0:L]
        for w in range(1, CONV_WIDTH):
            acc = acc + w_ref[w:w + 1, :] * taps[w * L:(w + 1) * L]
        return _silu(acc)

    xs = conv(extx, cwx_ref, cbx_ref)
    bc = conv(extb, cwb_ref, cbb_ref)
    extx[0:H0, :] = extx[L:L + H0, :]
    extb[0:H0, :] = extb[L:L + H0, :]

    dtr = dt_ref[...] + dtb_ref[...]
    dt = jnp.maximum(dtr, 0.0) + jnp.log1p(jnp.exp(-jnp.abs(dtr)))
    a = dt * (-jnp.exp(alog_ref[...]))

    row = lax.broadcasted_iota(jnp.int32, (L, L), 0)
    col = lax.broadcasted_iota(jnp.int32, (L, L), 1)
    tril = row >= col
    tri = jnp.where(tril, 1.0, 0.0).astype(BF16)
    a3 = jnp.concatenate(_split_bf16(a, 3), axis=1)
    cs3 = jnp.dot(tri, a3, preferred_element_type=F32)
    cs = (cs3[:, 0:LANES] + cs3[:, LANES:2 * LANES]) + cs3[:, 2 * LANES:3 * LANES]
    cs_t = cs.T

    er = lax.broadcasted_iota(jnp.int32, (LANES, D_SSM), 0)
    ec = lax.broadcasted_iota(jnp.int32, (LANES, D_SSM), 1)
    expand = jnp.where(ec // SSM_HEAD_DIM == er, 1.0, 0.0).astype(BF16)
    stack = jnp.concatenate(_split_bf16(dt, 2) + _split_bf16(cs, 2), axis=0)
    ex = jnp.dot(stack, expand, preferred_element_type=F32)
    dt_x = ex[0:L] + ex[L:2 * L]
    cs_x = ex[2 * L:3 * L] + ex[3 * L:4 * L]
    cs_last = cs_x[L - 1:L, :]

    xdt = xs * dt_x
    xdt_b = xdt.astype(BF16)
    xdec_b = (xdt * jnp.exp(cs_last - cs_x)).astype(BF16)
    dec_out = jnp.exp(cs_x)
    dec_chunk = jnp.exp(cs_last)

    lane = lax.broadcasted_iota(jnp.int32, (L, LANES), 1)
    low_half = lane < SSM_HEAD_DIM
    neg_inf = jnp.float32(-jnp.inf)

    y_parts = []
    for g in range(SSM_GROUPS):
        gsl = slice(g * GROUP_W, (g + 1) * GROUP_W)
        b_g = bc[:, g * SSM_STATE:(g + 1) * SSM_STATE].astype(BF16)
        c_g = bc[:, BC_W // 2 + g * SSM_STATE:BC_W // 2 + (g + 1) * SSM_STATE].astype(BF16)
        cb = lax.dot_general(c_g, b_g, (((1,), (1,)), ((), ())),
                             preferred_element_type=F32)
        s_in = state[g]
        y_off = jnp.dot(c_g, s_in.astype(BF16), preferred_element_type=F32) * dec_out[:, gsl]
        s_new = lax.dot_general(b_g, xdec_b[:, gsl], (((0,), (0,)), ((), ())),
                                preferred_element_type=F32)
        state[g] = s_in * dec_chunk[:, gsl] + s_new

        pair_out = []
        for pr in range(SSM_HPG // 2):
            ms = []
            for r in (2 * pr, 2 * pr + 1):
                h = g * SSM_HPG + r
                diff = cs[:, h:h + 1] - cs_t[h:h + 1, :]
                decay = jnp.exp(jnp.where(tril, diff, neg_inf))
                ms.append((cb * decay).astype(BF16))
            m2 = jnp.concatenate(ms, axis=1)
            c0 = g * GROUP_W + pr * LANES
            xp = xdt_b[:, c0:c0 + LANES]
            zero = jnp.zeros_like(xp)
            rhs = jnp.concatenate([jnp.where(low_half, xp, zero),
                                   jnp.where(low_half, zero, xp)], axis=0)
            pair_out.append(jnp.dot(m2, rhs, preferred_element_type=F32))
        y_parts.append(jnp.concatenate(pair_out, axis=1) + y_off)
    y = jnp.concatenate(y_parts, axis=1)

    y = y + dsk_ref[...] * xs
    y = y * _silu(z_ref[...].astype(F32))
    outs = []
    for g in range(SSM_GROUPS):
        gsl = slice(g * GROUP_W, (g + 1) * GROUP_W)
        yg = y[:, gsl]
        ms = jnp.mean(yg * yg, axis=-1, keepdims=True)
        outs.append((yg * lax.rsqrt(ms + NORM_EPS)) * nw_ref[:, gsl])
    y_ref[...] = jnp.concatenate(outs, axis=1).astype(y_ref.dtype)


def _ssd(proj3, dt3, cwx, cwb, cbx, cbb, dtb, alog, dsk, nw):
    b, s, _ = proj3.shape
    nc = s // CHUNK
    full = lambda shape: pl.BlockSpec(shape, lambda bi, ci: (0, 0))
    return pl.pallas_call(
        _ssd_kernel,
        grid=(b, nc),
        in_specs=[
            pl.BlockSpec((None, CHUNK, D_SSM), lambda bi, ci: (bi, ci, OFF_XS // D_SSM)),
            pl.BlockSpec((None, CHUNK, BC_W), lambda bi, ci: (bi, ci, OFF_BC // BC_W)),
            pl.BlockSpec((None, CHUNK, DT_PAD), lambda bi, ci: (bi, ci, 0)),
            pl.BlockSpec((None, CHUNK, D_SSM), lambda bi, ci: (bi, ci, OFF_ZS // D_SSM)),
            full((CONV_WIDTH, D_SSM)), full((CONV_WIDTH, BC_W)),
            full((1, D_SSM)), full((1, BC_W)),
            full((1, DT_PAD)), full((1, DT_PAD)),
            full((1, D_SSM)), full((1, D_SSM)),
        ],
        out_specs=pl.BlockSpec((None, CHUNK, D_SSM), lambda bi, ci: (bi, ci, 0)),
        out_shape=jax.ShapeDtypeStruct((b, s, D_SSM), BF16),
        scratch_shapes=[
            pltpu.VMEM((BF16_SUBLANES + CHUNK, D_SSM), BF16),
            pltpu.VMEM((BF16_SUBLANES + CHUNK, BC_W), BF16),
            pltpu.VMEM((CONV_WIDTH * CHUNK, BF16_SUBLANES + CHUNK), BF16),
            pltpu.VMEM((SSM_GROUPS, SSM_STATE, GROUP_W), F32),
        ],
        compiler_params=pltpu.CompilerParams(
            dimension_semantics=("parallel", "arbitrary"),
            vmem_limit_bytes=VMEM_LIMIT),
        name="ssd",
    )(proj3, proj3, dt3, proj3, cwx, cwb, cbx, cbb, dtb, alog, dsk, nw)


def _attn_kernel(lam_init, q_ref, k_ref, v_ref, z_ref, lamv_ref, sw_ref, o_ref,
                 vt_scr, bias_scr, s_scr, p_scr, acc_scr):
    i = pl.program_id(2)
    tq = tk = AT_T
    nkb = vt_scr.shape[1]
    heads = range(AT_HPS)
    streams = range(2 * AT_HPS)
    hcols = lambda hh: slice(hh * ATTN_VDIM, (hh + 1) * ATTN_VDIM)

    @pl.when(i == 0)
    def _():
        kpos = lax.broadcasted_iota(jnp.int32, (tk, tq), 0)
        qpos = lax.broadcasted_iota(jnp.int32, (tk, tq), 1)
        bias_scr[...] = jnp.where(kpos <= qpos, 0.0, -jnp.inf).astype(F32)
        ones = jnp.ones((AT_ACC_ROWS - ATTN_VDIM, tk), BF16)
        for hh in heads:
            for cblk in range(nkb):
                vb = v_ref[cblk * tk:(cblk + 1) * tk, hcols(hh)].astype(F32)
                vt_scr[hh, cblk, 0:ATTN_VDIM, :] = vb.T.astype(BF16)
                vt_scr[hh, cblk, ATTN_VDIM:AT_ACC_ROWS, :] = ones

    sub = lax.broadcasted_iota(jnp.int32, (ATTN_VDIM, tq), 0)
    q_t = []
    for hh in heads:
        qh_t = q_ref[:, hcols(hh)].astype(F32).T
        q_t.append(jnp.where(sub < ATTN_HEAD_DIM, qh_t, 0.0).astype(BF16))
        q_t.append(jnp.where(sub >= ATTN_HEAD_DIM, qh_t, 0.0).astype(BF16))

    acc_scr[...] = jnp.zeros(acc_scr.shape, F32)
    p_scr[...] = jnp.zeros(p_scr.shape, BF16)
    neg_inf = jnp.float32(-jnp.inf)

    def scores(j, n):
        start = pl.multiple_of(j * tk, tk)
        k_b = k_ref[pl.ds(start, tk), hcols(n // 2)]
        s_scr[n] = jnp.dot(k_b, q_t[n], preferred_element_type=F32)

    def softmax(m, n):
        s = s_scr[n]
        m_new = jnp.maximum(m, jnp.max(s, axis=0, keepdims=True))
        p_scr[n] = jnp.exp2(s - m_new).astype(BF16)
        return m_new, jnp.exp2(m - m_new)

    def accumulate(j, n, alpha):
        vt_b = vt_scr[n // 2, jnp.maximum(j, 0)]
        acc_scr[n] = alpha * acc_scr[n] + jnp.dot(
            vt_b, p_scr[n], preferred_element_type=F32)

    def step(j, carry):
        ms, alphas = carry
        new = []
        for n in streams:
            accumulate(j - 1, n, alphas[n])
            new.append(softmax(ms[n], n))
            scores(j + 1, n)
        return tuple(x[0] for x in new), tuple(x[1] for x in new)

    row = lambda v: jnp.full((1, tq), v, F32)
    carry = (tuple(row(neg_inf) for _ in streams), tuple(row(1.0) for _ in streams))
    for n in streams:
        scores(0, n)
    carry = lax.fori_loop(0, i, lambda j, cr: step(j, cr), carry)
    def diag_tiles():
        for c0 in range(0, tq, AT_DW):
            yield slice(0, c0 + AT_DW), slice(c0, c0 + AT_DW)

    def softmax_diag(m, n):
        alpha = []
        for rows, cols in diag_tiles():
            s = s_scr[n, rows, cols] + bias_scr[rows, cols]
            m_old = m[:, cols]
            m_new = jnp.maximum(m_old, jnp.max(s, axis=0, keepdims=True))
            p_scr[n, rows, cols] = jnp.exp2(s - m_new).astype(BF16)
            alpha.append(jnp.exp2(m_old - m_new))
        return jnp.concatenate(alpha, axis=1)

    def accumulate_diag(n, alpha):
        for rows, cols in diag_tiles():
            acc_scr[n, :, cols] = alpha[:, cols] * acc_scr[n, :, cols] + jnp.dot(
                vt_scr[n // 2, i, :, rows], p_scr[n, rows, cols], preferred_element_type=F32)

    ms, alphas = carry
    for n in streams:
        accumulate(i - 1, n, alphas[n])
        accumulate_diag(n, softmax_diag(ms[n], n))

    lv = lamv_ref[...]
    lam = (jnp.exp(jnp.sum(lv[0:1] * lv[1:2], axis=-1, keepdims=True))
           - jnp.exp(jnp.sum(lv[2:3] * lv[3:4], axis=-1, keepdims=True))
           + lam_init)
    for hh in heads:
        n1, n2 = 2 * hh, 2 * hh + 1
        l1 = acc_scr[n1, ATTN_VDIM:ATTN_VDIM + 1, :]
        l2 = acc_scr[n2, ATTN_VDIM:ATTN_VDIM + 1, :]
        o_t = (acc_scr[n1, 0:ATTN_VDIM, :] * (1.0 / l1)
               - lam * (acc_scr[n2, 0:ATTN_VDIM, :] * (1.0 / l2)))
        o = o_t.T
        ms = jnp.mean(o * o, axis=-1, keepdims=True)
        on = (o * lax.rsqrt(ms + NORM_EPS)) * sw_ref[...] * (1.0 - lam_init)
        gate = _silu(z_ref[:, hcols(hh)].astype(F32))
        o_ref[:, hcols(hh)] = (on * gate).astype(o_ref.dtype)


def _diff_attn(proj3, lamv, subln_w, lam_init):
    b, s, _ = proj3.shape
    nq = s // AT_T
    gw = AT_HPS * ATTN_VDIM
    gb = lambda off: off // gw
    nstream = 2 * AT_HPS
    return pl.pallas_call(
        functools.partial(_attn_kernel, lam_init),
        grid=(b, ATTN_HEADS // AT_HPS, nq),
        in_specs=[
            pl.BlockSpec((None, AT_T, gw), lambda bi, g, i: (bi, i, gb(OFF_Q) + g)),
            pl.BlockSpec((None, s, gw), lambda bi, g, i: (bi, 0, gb(OFF_K) + g)),
            pl.BlockSpec((None, s, gw), lambda bi, g, i: (bi, 0, gb(OFF_V) + g)),
            pl.BlockSpec((None, AT_T, gw), lambda bi, g, i: (bi, i, gb(OFF_ZA) + g)),
            pl.BlockSpec((4, ATTN_HEAD_DIM), lambda bi, g, i: (0, 0)),
            pl.BlockSpec((1, ATTN_VDIM), lambda bi, g, i: (0, 0)),
        ],
        out_specs=pl.BlockSpec((None, AT_T, gw), lambda bi, g, i: (bi, i, g)),
        out_shape=jax.ShapeDtypeStruct((b, s, D_ATTN), BF16),
        scratch_shapes=[
            pltpu.VMEM((AT_HPS, s // AT_T, AT_ACC_ROWS, AT_T), BF16),
            pltpu.VMEM((AT_T, AT_T), F32),
            pltpu.VMEM((nstream, AT_T, AT_T), F32),
            pltpu.VMEM((nstream, AT_T, AT_T), BF16),
            pltpu.VMEM((nstream, AT_ACC_ROWS, AT_T), F32),
        ],
        compiler_params=pltpu.CompilerParams(
            dimension_semantics=("parallel", "parallel", "arbitrary"),
            vmem_limit_bytes=VMEM_LIMIT),
        name="diffattn",
    )(proj3, proj3, proj3, proj3, lamv, subln_w)


def _out_proj_kernel(ys_ref, ya_ref, w_ref, x_ref, nw_ref, o_ref):
    ys = ys_ref[...]
    ya = ya_ref[...]
    ssq = jnp.zeros((OUT_TM, 1), F32)
    for jj in range(D_MODEL // OUT_TN):
        sl = slice(jj * OUT_TN, (jj + 1) * OUT_TN)
        acc = (jnp.dot(ys, w_ref[0:D_SSM, sl], preferred_element_type=F32)
               + jnp.dot(ya, w_ref[D_SSM:D_SSM + D_ATTN, sl], preferred_element_type=F32))
        ssq = ssq + jnp.sum(acc * acc, axis=-1, keepdims=True)
        o_ref[:, sl] = acc
    inv = lax.rsqrt(ssq * (1.0 / D_MODEL) + NORM_EPS)
    for r in range(OUT_TM // OUT_RC):
        rows = slice(r * OUT_RC, (r + 1) * OUT_RC)
        inv_r = inv[rows]
        for jj in range(D_MODEL // OUT_TN):
            sl = slice(jj * OUT_TN, (jj + 1) * OUT_TN)
            o_ref[rows, sl] = x_ref[rows, sl] + (o_ref[rows, sl] * inv_r) * nw_ref[:, sl]


def _out_proj(ys, ya, w_o, x2, norm_w):
    m = x2.shape[0]
    return pl.pallas_call(
        _out_proj_kernel,
        grid=(m // OUT_TM,),
        in_specs=[
            pl.BlockSpec((OUT_TM, D_SSM), lambda i: (i, 0)),
            pl.BlockSpec((OUT_TM, D_ATTN), lambda i: (i, 0)),
            pl.BlockSpec((D_SSM + D_ATTN, D_MODEL), lambda i: (0, 0),
                         pipeline_mode=pl.Buffered(1)),
            pl.BlockSpec((OUT_TM, D_MODEL), lambda i: (i, 0)),
            pl.BlockSpec((1, D_MODEL), lambda i: (0, 0)),
        ],
        out_specs=pl.BlockSpec((OUT_TM, D_MODEL), lambda i: (i, 0)),
        out_shape=jax.ShapeDtypeStruct((m, D_MODEL), F32),
        compiler_params=pltpu.CompilerParams(
            dimension_semantics=("parallel",),
            vmem_limit_bytes=OUT_VMEM_LIMIT),
        name="out_proj",
    )(ys, ya, w_o, x2, norm_w)


def _layer(x, layer, pre_norm_w, post_norm_w, w_in, conv_w, conv_b, dt_bias, a_log, d_skip,
           ssm_norm_w, lambda_q1, lambda_k1, lambda_q2, lambda_k2, attn_subln_w, w_out):
    b, s, d = x.shape
    x2 = x.reshape(b * s, d)

    wt_main = w_in.T.astype(BF16)
    wt_dt = jnp.pad(wt_main[W_IN_DT:W_IN_QKV], ((0, DT_PAD - SSM_HEADS), (0, 0)))

    q_scale = ATTN_HEAD_DIM ** -0.5 * math.log2(math.e)
    col_scale = jnp.ones((1, PROJ_W), F32).at[:, OFF_Q:OFF_Q + D_ATTN].set(q_scale)
    proj, dt_raw = _in_proj(x2, pre_norm_w.reshape(1, d), wt_main, wt_dt, col_scale)
    proj3 = proj.reshape(b, s, PROJ_W)
    dt3 = dt_raw.reshape(b, s, DT_PAD)

    pad_h = lambda v: jnp.pad(v.reshape(1, SSM_HEADS), ((0, 0), (0, DT_PAD - SSM_HEADS)))
    y_ssm = _ssd(
        proj3, dt3,
        conv_w[:, :D_SSM], conv_w[:, D_SSM:],
        conv_b[:D_SSM].reshape(1, D_SSM), conv_b[D_SSM:].reshape(1, BC_W),
        pad_h(dt_bias), pad_h(a_log),
        jnp.repeat(d_skip, SSM_HEAD_DIM).reshape(1, D_SSM),
        ssm_norm_w.reshape(1, D_SSM))

    lam_init = 0.8 - 0.6 * math.exp(-0.3 * layer)
    lamv = jnp.stack([lambda_q1, lambda_k1, lambda_q2, lambda_k2], axis=0)
    y_attn = _diff_attn(proj3, lamv, attn_subln_w.reshape(1, ATTN_VDIM), lam_init)

    out = _out_proj(y_ssm.reshape(b * s, D_SSM), y_attn.reshape(b * s, D_ATTN),
                    w_out.astype(BF16), x2, post_norm_w.reshape(1, d))
    return out.reshape(b, s, d)


def kernel(x, pre_norm_w, post_norm_w, w_in, conv_w, conv_b, dt_bias, a_log, d_skip, ssm_norm_w, lambda_q1, lambda_k1, lambda_q2, lambda_k2, attn_subln_w, w_out):
    for layer in range(w_in.shape[0]):
        x = _layer(x, layer, pre_norm_w[layer], post_norm_w[layer], w_in[layer],
                   conv_w[layer], conv_b[layer], dt_bias[layer], a_log[layer], d_skip[layer],
                   ssm_norm_w[layer], lambda_q1[layer], lambda_k1[layer], lambda_q2[layer],
                   lambda_k2[layer], attn_subln_w[layer], w_out[layer])
    return x
```

```python
import functools
import math

import jax
import jax.numpy as jnp
from jax import lax
from jax.experimental import pallas as pl
from jax.experimental.pallas import tpu as pltpu

F32 = jnp.float32
BF16 = jnp.bfloat16

D_MODEL = 4096
D_SSM = 2048
D_ATTN = 2048
SSM_HEAD_DIM = 64
SSM_HEADS = 32
SSM_GROUPS = 4
SSM_HPG = 8
SSM_STATE = 128
CONV_WIDTH = 4
CHUNK = 128
ATTN_HEAD_DIM = 64
ATTN_VDIM = 128
ATTN_HEADS = 16
NORM_EPS = 1e-6
GROUP_W = D_SSM // SSM_GROUPS
BC_W = 2 * SSM_GROUPS * SSM_STATE

LANES = 128
SUBLANES = 8
VMEM_LIMIT = 56 * 1024 * 1024

OFF_ZS = 0
OFF_ZA = 2048
OFF_XS = 4096
OFF_BC = 6144
OFF_Q = 7168
OFF_K = 9216
OFF_V = 11264
PROJ_W = 13312
W_IN_DT = OFF_Q
W_IN_QKV = W_IN_DT + SSM_HEADS
DT_PAD = LANES

IN_TM, IN_TN = 1024, 1024
IN_XR = 512
IN_NP = IN_TM // IN_XR
AT_T = 512
SSD_BPS = 2
AT_DW = 256
AT_HPS = 4
BF16_SUBLANES = 16
AT_ACC_ROWS = ATTN_VDIM + BF16_SUBLANES
OUT_TM, OUT_TN, OUT_RC = 256, 512, 64
OUT_VMEM_LIMIT = 60 * 1024 * 1024


def _silu(x):
    hx = 0.5 * x
    return hx + hx * jnp.tanh(hx)


def _dot_nt(a, b_t):
    return lax.dot_general(a, b_t, (((1,), (1,)), ((), ())), preferred_element_type=F32)


def _in_proj_kernel(x_ref, nw_ref, wt_ref, wdt_ref, cs_ref, o_ref, dt_ref, h_scr):
    s = pl.program_id(1)

    for p in range(IN_NP):
        @pl.when(s == p)
        def _(p=p):
            x = x_ref[...]
            ms = jnp.mean(x * x, axis=-1, keepdims=True)
            h = (x * lax.rsqrt(ms + NORM_EPS)) * nw_ref[...]
            hb = h.astype(BF16)
            h_scr[p * IN_XR:(p + 1) * IN_XR, :] = hb
            dt_ref[...] = _dot_nt(hb, wdt_ref[...])

    @pl.when(s >= IN_NP)
    def _():
        o_ref[...] = (_dot_nt(h_scr[...], wt_ref[...]) * cs_ref[...]).astype(o_ref.dtype)


def _in_proj(x2, norm_w, wt_main, wt_dt, col_scale):
    m = x2.shape[0]
    xrow = lambda i, s: i * IN_NP + jnp.minimum(s, IN_NP - 1)
    col = lambda s: jnp.maximum(s - IN_NP, 0)
    return pl.pallas_call(
        _in_proj_kernel,
        grid=(m // IN_TM, IN_NP + PROJ_W // IN_TN),
        in_specs=[
            pl.BlockSpec((IN_XR, D_MODEL), lambda i, s: (xrow(i, s), 0)),
            pl.BlockSpec((1, D_MODEL), lambda i, s: (0, 0)),
            pl.BlockSpec((pl.Element(IN_TN), pl.Element(D_MODEL)),
                         lambda i, s: (BF16_SUBLANES * (col(s) * (IN_TN // BF16_SUBLANES) + jnp.where(
                             col(s) * IN_TN >= W_IN_DT, SSM_HEADS // BF16_SUBLANES, 0)), 0)),
            pl.BlockSpec((DT_PAD, D_MODEL), lambda i, s: (0, 0)),
            pl.BlockSpec((1, IN_TN), lambda i, s: (0, col(s))),
        ],
        out_specs=[
            pl.BlockSpec((IN_TM, IN_TN), lambda i, s: (i, col(s))),
            pl.BlockSpec((IN_XR, DT_PAD), lambda i, s: (xrow(i, s), 0)),
        ],
        out_shape=[
            jax.ShapeDtypeStruct((m, PROJ_W), BF16),
            jax.ShapeDtypeStruct((m, DT_PAD), F32),
        ],
        scratch_shapes=[pltpu.VMEM((IN_TM, D_MODEL), BF16)],
        compiler_params=pltpu.CompilerParams(
            dimension_semantics=("parallel", "arbitrary"),
            vmem_limit_bytes=VMEM_LIMIT),
        name="in_proj",
    )(x2, norm_w, wt_main, wt_dt, col_scale)


def _split_bf16(x, terms):
    parts = []
    r = x
    for _ in range(terms):
        p = r.astype(BF16)
        parts.append(p)
        r = r - p.astype(F32)
    return parts


def _ssd_kernel(xs_ref, bc_ref, dt_ref, z_ref, cwx_ref, cwb_ref, cbx_ref, cbb_ref,
                dtb_ref, alog_ref, dsk_ref, nw_ref, y_ref, extx, extb, shift, expand, state):
    L = CHUNK
    H0 = BF16_SUBLANES

    @pl.when(pl.program_id(1) == 0)
    def _():
        extx[:, 0:H0, :] = jnp.zeros((SSD_BPS, H0, D_SSM), BF16)
        extb[:, 0:H0, :] = jnp.zeros((SSD_BPS, H0, BC_W), BF16)
        state[...] = jnp.zeros(state.shape, F32)
        out_row = lax.broadcasted_iota(jnp.int32, shift.shape, 0)
        src_row = lax.broadcasted_iota(jnp.int32, shift.shape, 1)
        want = H0 + out_row % L - (CONV_WIDTH - 1) + out_row // L
        shift[...] = jnp.where(src_row == want, 1.0, 0.0).astype(BF16)
        er = lax.broadcasted_iota(jnp.int32, expand.shape, 0)
        ec = lax.broadcasted_iota(jnp.int32, expand.shape, 1)
        expand[...] = jnp.where(ec // SSM_HEAD_DIM == er, 1.0, 0.0).astype(BF16)

    for bb in range(SSD_BPS):
        _ssd_chunk(xs_ref.at[bb], bc_ref.at[bb], dt_ref.at[bb], z_ref.at[bb], cwx_ref, cwb_ref,
                   cbx_ref, cbb_ref, dtb_ref, alog_ref, dsk_ref, nw_ref, y_ref.at[bb],
                   extx.at[bb], extb.at[bb], shift, expand, state.at[bb])


def _ssd_chunk(xs_ref, bc_ref, dt_ref, z_ref, cwx_ref, cwb_ref, cbx_ref, cbb_ref,
               dtb_ref, alog_ref, dsk_ref, nw_ref, y_ref, extx, extb, shift, expand, state):
    L = CHUNK
    H0 = BF16_SUBLANES

    extx[H0:H0 + L, :] = xs_ref[...]
    extb[H0:H0 + L, :] = bc_ref[...]

    def conv(ext, w_ref, b_ref):
        taps = jnp.dot(shift[...], ext[...], preferred_element_type=F32)
        acc = b_ref[...] + w_ref[0:1, :] * taps[0:L]
        for w in range(1, CONV_WIDTH):
            acc = acc + w_ref[w:w + 1, :] * taps[w * L:(w + 1) * L]
        return _silu(acc)

    xs = conv(extx, cwx_ref, cbx_ref)
    bc = conv(extb, cwb_ref, cbb_ref)
    extx[0:H0, :] = extx[L:L + H0, :]
    extb[0:H0, :] = extb[L:L + H0, :]

    dtr = dt_ref[...] + dtb_ref[...]
    dt = jnp.maximum(dtr, 0.0) + jnp.log1p(jnp.exp(-jnp.abs(dtr)))
    a = dt * (-jnp.exp(alog_ref[...]))

    row = lax.broadcasted_iota(jnp.int32, (L, L), 0)
    col = lax.broadcasted_iota(jnp.int32, (L, L), 1)
    tril = row >= col
    tri = jnp.where(tril, 1.0, 0.0).astype(BF16)
    a3 = jnp.concatenate(_split_bf16(a, 3), axis=1)
    cs3 = jnp.dot(tri, a3, preferred_element_type=F32)
    cs = (cs3[:, 0:LANES] + cs3[:, LANES:2 * LANES]) + cs3[:, 2 * LANES:3 * LANES]
    cs_t = cs.T

    stack = jnp.concatenate(_split_bf16(dt, 2) + _split_bf16(cs, 2), axis=0)
    ex = jnp.dot(stack, expand[...], preferred_element_type=F32)
    dt_x = ex[0:L] + ex[L:2 * L]
    cs_x = ex[2 * L:3 * L] + ex[3 * L:4 * L]
    cs_last = cs_x[L - 1:L, :]

    xdt = xs * dt_x
    xdt_b = xdt.astype(BF16)
    xdec_b = (xdt * jnp.exp(cs_last - cs_x)).astype(BF16)
    dec_out = jnp.exp(cs_x)
    dec_chunk = jnp.exp(cs_last)

    lane = lax.broadcasted_iota(jnp.int32, (L, LANES), 1)
    low_half = lane < SSM_HEAD_DIM
    neg_inf = jnp.float32(-jnp.inf)

    y_parts = []
    for g in range(SSM_GROUPS):
        gsl = slice(g * GROUP_W, (g + 1) * GROUP_W)
        b_g = bc[:, g * SSM_STATE:(g + 1) * SSM_STATE].astype(BF16)
        c_g = bc[:, BC_W // 2 + g * SSM_STATE:BC_W // 2 + (g + 1) * SSM_STATE].astype(BF16)
        cb = lax.dot_general(c_g, b_g, (((1,), (1,)), ((), ())),
                             preferred_element_type=F32)
        s_in = state[g]
        y_off = jnp.dot(c_g, s_in.astype(BF16), preferred_element_type=F32) * dec_out[:, gsl]
        s_new = lax.dot_general(b_g, xdec_b[:, gsl], (((0,), (0,)), ((), ())),
                                preferred_element_type=F32)
        state[g] = s_in * dec_chunk[:, gsl] + s_new

        pair_out = []
        for pr in range(SSM_HPG // 2):
            ms = []
            for r in (2 * pr, 2 * pr + 1):
                h = g * SSM_HPG + r
                diff = cs[:, h:h + 1] - cs_t[h:h + 1, :]
                decay = jnp.exp(jnp.where(tril, diff, neg_inf))
                ms.append((cb * decay).astype(BF16))
            m2 = jnp.concatenate(ms, axis=1)
            c0 = g * GROUP_W + pr * LANES
            xp = xdt_b[:, c0:c0 + LANES]
            zero = jnp.zeros_like(xp)
            rhs = jnp.concatenate([jnp.where(low_half, xp, zero),
                                   jnp.where(low_half, zero, xp)], axis=0)
            pair_out.append(jnp.dot(m2, rhs, preferred_element_type=F32))
        y_parts.append(jnp.concatenate(pair_out, axis=1) + y_off)
    y = jnp.concatenate(y_parts, axis=1)

    y = y + dsk_ref[...] * xs
    y = y * _silu(z_ref[...].astype(F32))
    outs = []
    for g in range(SSM_GROUPS):
        gsl = slice(g * GROUP_W, (g + 1) * GROUP_W)
        yg = y[:, gsl]
        ms = jnp.mean(yg * yg, axis=-1, keepdims=True)
        outs.append((yg * lax.rsqrt(ms + NORM_EPS)) * nw_ref[:, gsl])
    y_ref[...] = jnp.concatenate(outs, axis=1).astype(y_ref.dtype)


def _ssd(proj3, dt3, cwx, cwb, cbx, cbb, dtb, alog, dsk, nw):
    b, s, _ = proj3.shape
    nc = s // CHUNK
    full = lambda shape: pl.BlockSpec(shape, lambda bi, ci: (0, 0))
    return pl.pallas_call(
        _ssd_kernel,
        grid=(b // SSD_BPS, nc),
        in_specs=[
            pl.BlockSpec((SSD_BPS, CHUNK, D_SSM), lambda bi, ci: (bi, ci, OFF_XS // D_SSM)),
            pl.BlockSpec((SSD_BPS, CHUNK, BC_W), lambda bi, ci: (bi, ci, OFF_BC // BC_W)),
            pl.BlockSpec((SSD_BPS, CHUNK, DT_PAD), lambda bi, ci: (bi, ci, 0)),
            pl.BlockSpec((SSD_BPS, CHUNK, D_SSM), lambda bi, ci: (bi, ci, OFF_ZS // D_SSM)),
            full((CONV_WIDTH, D_SSM)), full((CONV_WIDTH, BC_W)),
            full((1, D_SSM)), full((1, BC_W)),
            full((1, DT_PAD)), full((1, DT_PAD)),
            full((1, D_SSM)), full((1, D_SSM)),
        ],
        out_specs=pl.BlockSpec((SSD_BPS, CHUNK, D_SSM), lambda bi, ci: (bi, ci, 0)),
        out_shape=jax.ShapeDtypeStruct((b, s, D_SSM), BF16),
        scratch_shapes=[
            pltpu.VMEM((SSD_BPS, BF16_SUBLANES + CHUNK, D_SSM), BF16),
            pltpu.VMEM((SSD_BPS, BF16_SUBLANES + CHUNK, BC_W), BF16),
            pltpu.VMEM((CONV_WIDTH * CHUNK, BF16_SUBLANES + CHUNK), BF16),
            pltpu.VMEM((LANES, D_SSM), BF16),
            pltpu.VMEM((SSD_BPS, SSM_GROUPS, SSM_STATE, GROUP_W), F32),
        ],
        compiler_params=pltpu.CompilerParams(
            dimension_semantics=("parallel", "arbitrary"),
            vmem_limit_bytes=VMEM_LIMIT),
        name="ssd",
    )(proj3, proj3, dt3, proj3, cwx, cwb, cbx, cbb, dtb, alog, dsk, nw)


def _attn_kernel(lam_init, q_ref, k_ref, v_ref, z_ref, lamv_ref, sw_ref, o_ref,
                 vt_scr, bias_scr, s_scr, p_scr, acc_scr):
    i = pl.program_id(2)
    tq = tk = AT_T
    nkb = vt_scr.shape[1]
    heads = range(AT_HPS)
    streams = range(2 * AT_HPS)
    hcols = lambda hh: slice(hh * ATTN_VDIM, (hh + 1) * ATTN_VDIM)

    @pl.when(i == 0)
    def _():
        kpos = lax.broadcasted_iota(jnp.int32, (tk, tq), 0)
        qpos = lax.broadcasted_iota(jnp.int32, (tk, tq), 1)
        bias_scr[...] = jnp.where(kpos <= qpos, 0.0, -jnp.inf).astype(F32)
        ones = jnp.ones((AT_ACC_ROWS - ATTN_VDIM, tk), BF16)
        for hh in heads:
            for cblk in range(nkb):
                vb = v_ref[cblk * tk:(cblk + 1) * tk, hcols(hh)].astype(F32)
                vt_scr[hh, cblk, 0:ATTN_VDIM, :] = vb.T.astype(BF16)
                vt_scr[hh, cblk, ATTN_VDIM:AT_ACC_ROWS, :] = ones

    sub = lax.broadcasted_iota(jnp.int32, (ATTN_VDIM, tq), 0)
    q_t = []
    for hh in heads:
        qh_t = q_ref[:, hcols(hh)].astype(F32).T
        q_t.append(jnp.where(sub < ATTN_HEAD_DIM, qh_t, 0.0).astype(BF16))
        q_t.append(jnp.where(sub >= ATTN_HEAD_DIM, qh_t, 0.0).astype(BF16))

    acc_scr[...] = jnp.zeros(acc_scr.shape, F32)
    p_scr[...] = jnp.zeros(p_scr.shape, BF16)
    neg_inf = jnp.float32(-jnp.inf)

    def scores(j, n):
        start = pl.multiple_of(j * tk, tk)
        k_b = k_ref[pl.ds(start, tk), hcols(n // 2)]
        s_scr[n] = jnp.dot(k_b, q_t[n], preferred_element_type=F32)

    def softmax(m, n):
        s = s_scr[n]
        m_new = jnp.maximum(m, jnp.max(s, axis=0, keepdims=True))
        p_scr[n] = jnp.exp2(s - m_new).astype(BF16)
        return m_new, jnp.exp2(m - m_new)

    def accumulate(j, n, alpha):
        vt_b = vt_scr[n // 2, jnp.maximum(j, 0)]
        acc_scr[n] = alpha * acc_scr[n] + jnp.dot(
            vt_b, p_scr[n], preferred_element_type=F32)

    def step(j, carry):
        ms, alphas = carry
        new = []
        for n in streams:
            accumulate(j - 1, n, alphas[n])
            new.append(softmax(ms[n], n))
            scores(j + 1, n)
        return tuple(x[0] for x in new), tuple(x[1] for x in new)

    row = lambda v: jnp.full((1, tq), v, F32)
    carry = (tuple(row(neg_inf) for _ in streams), tuple(row(1.0) for _ in streams))
    for n in streams:
        scores(0, n)
    carry = lax.fori_loop(0, i, lambda j, cr: step(j, cr), carry)
    def diag_tiles():
        for c0 in range(0, tq, AT_DW):
            yield slice(0, c0 + AT_DW), slice(c0, c0 + AT_DW)

    def softmax_diag(m, n):
        alpha = []
        for rows, cols in diag_tiles():
            s = s_scr[n, rows, cols] + bias_scr[rows, cols]
            m_old = m[:, cols]
            m_new = jnp.maximum(m_old, jnp.max(s, axis=0, keepdims=True))
            p_scr[n, rows, cols] = jnp.exp2(s - m_new).astype(BF16)
            alpha.append(jnp.exp2(m_old - m_new))
        return jnp.concatenate(alpha, axis=1)

    def accumulate_diag(n, alpha):
        for rows, cols in diag_tiles():
            acc_scr[n, :, cols] = alpha[:, cols] * acc_scr[n, :, cols] + jnp.dot(
                vt_scr[n // 2, i, :, rows], p_scr[n, rows, cols], preferred_element_type=F32)

    ms, alphas = carry
    for n in streams:
        accumulate(i - 1, n, alphas[n])
        accumulate_diag(n, softmax_diag(ms[n], n))

    lv = lamv_ref[...]
    lam = (jnp.exp(jnp.sum(lv[0:1] * lv[1:2], axis=-1, keepdims=True))
           - jnp.exp(jnp.sum(lv[2:3] * lv[3:4], axis=-1, keepdims=True))
           + lam_init)
    for hh in heads:
        n1, n2 = 2 * hh, 2 * hh + 1
        l1 = acc_scr[n1, ATTN_VDIM:ATTN_VDIM + 1, :]
        l2 = acc_scr[n2, ATTN_VDIM:ATTN_VDIM + 1, :]
        o_t = (acc_scr[n1, 0:ATTN_VDIM, :] * (1.0 / l1)
               - lam * (acc_scr[n2, 0:ATTN_VDIM, :] * (1.0 / l2)))
        o = o_t.T
        ms = jnp.mean(o * o, axis=-1, keepdims=True)
        on = (o * lax.rsqrt(ms + NORM_EPS)) * sw_ref[...] * (1.0 - lam_init)
        gate = _silu(z_ref[:, hcols(hh)].astype(F32))
        o_ref[:, hcols(hh)] = (on * gate).astype(o_ref.dtype)


def _diff_attn(proj3, lamv, subln_w, lam_init):
    b, s, _ = proj3.shape
    nq = s // AT_T
    gw = AT_HPS * ATTN_VDIM
    gb = lambda off: off // gw
    nstream = 2 * AT_HPS
    return pl.pallas_call(
        functools.partial(_attn_kernel, lam_init),
        grid=(b, ATTN_HEADS // AT_HPS, nq),
        in_specs=[
            pl.BlockSpec((None, AT_T, gw), lambda bi, g, i: (bi, i, gb(OFF_Q) + g)),
            pl.BlockSpec((None, s, gw), lambda bi, g, i: (bi, 0, gb(OFF_K) + g)),
            pl.BlockSpec((None, s, gw), lambda bi, g, i: (bi, 0, gb(OFF_V) + g)),
            pl.BlockSpec((None, AT_T, gw), lambda bi, g, i: (bi, i, gb(OFF_ZA) + g)),
            pl.BlockSpec((4, ATTN_HEAD_DIM), lambda bi, g, i: (0, 0)),
            pl.BlockSpec((1, ATTN_VDIM), lambda bi, g, i: (0, 0)),
        ],
        out_specs=pl.BlockSpec((None, AT_T, gw), lambda bi, g, i: (bi, i, g)),
        out_shape=jax.ShapeDtypeStruct((b, s, D_ATTN), BF16),
        scratch_shapes=[
            pltpu.VMEM((AT_HPS, s // AT_T, AT_ACC_ROWS, AT_T), BF16),
            pltpu.VMEM((AT_T, AT_T), F32),
            pltpu.VMEM((nstream, AT_T, AT_T), F32),
            pltpu.VMEM((nstream, AT_T, AT_T), BF16),
            pltpu.VMEM((nstream, AT_ACC_ROWS, AT_T), F32),
        ],
        compiler_params=pltpu.CompilerParams(
            dimension_semantics=("parallel", "parallel", "arbitrary"),
            vmem_limit_bytes=VMEM_LIMIT),
        name="diffattn",
    )(proj3, proj3, proj3, proj3, lamv, subln_w)


def _out_proj_kernel(ys_ref, ya_ref, w_ref, x_ref, nw_ref, o_ref):
    ys = ys_ref[...]
    ya = ya_ref[...]
    ssq = jnp.zeros((OUT_TM, 1), F32)
    for jj in range(D_MODEL // OUT_TN):
        sl = slice(jj * OUT_TN, (jj + 1) * OUT_TN)
        acc = (jnp.dot(ys, w_ref[0:D_SSM, sl], preferred_element_type=F32)
               + jnp.dot(ya, w_ref[D_SSM:D_SSM + D_ATTN, sl], preferred_element_type=F32))
        ssq = ssq + jnp.sum(acc * acc, axis=-1, keepdims=True)
        o_ref[:, sl] = acc
    inv = lax.rsqrt(ssq * (1.0 / D_MODEL) + NORM_EPS)
    for r in range(OUT_TM // OUT_RC):
        rows = slice(r * OUT_RC, (r + 1) * OUT_RC)
        inv_r = inv[rows]
        for jj in range(D_MODEL // OUT_TN):
            sl = slice(jj * OUT_TN, (jj + 1) * OUT_TN)
            o_ref[rows, sl] = x_ref[rows, sl] + (o_ref[rows, sl] * inv_r) * nw_ref[:, sl]


def _out_proj(ys, ya, w_o, x2, norm_w):
    m = x2.shape[0]
    return pl.pallas_call(
        _out_proj_kernel,
        grid=(m // OUT_TM,),
        in_specs=[
            pl.BlockSpec((OUT_TM, D_SSM), lambda i: (i, 0)),
            pl.BlockSpec((OUT_TM, D_ATTN), lambda i: (i, 0)),
            pl.BlockSpec((D_SSM + D_ATTN, D_MODEL), lambda i: (0, 0),
                         pipeline_mode=pl.Buffered(1)),
            pl.BlockSpec((OUT_TM, D_MODEL), lambda i: (i, 0)),
            pl.BlockSpec((1, D_MODEL), lambda i: (0, 0)),
        ],
        out_specs=pl.BlockSpec((OUT_TM, D_MODEL), lambda i: (i, 0)),
        out_shape=jax.ShapeDtypeStruct((m, D_MODEL), F32),
        compiler_params=pltpu.CompilerParams(
            dimension_semantics=("parallel",),
            vmem_limit_bytes=OUT_VMEM_LIMIT),
        name="out_proj",
    )(ys, ya, w_o, x2, norm_w)


def _layer(x, layer, pre_norm_w, post_norm_w, w_in, conv_w, conv_b, dt_bias, a_log, d_skip,
           ssm_norm_w, lambda_q1, lambda_k1, lambda_q2, lambda_k2, attn_subln_w, w_out):
    b, s, d = x.shape
    x2 = x.reshape(b * s, d)

    wt_main = w_in.T.astype(BF16)
    wt_dt = jnp.pad(wt_main[W_IN_DT:W_IN_QKV], ((0, DT_PAD - SSM_HEADS), (0, 0)))

    q_scale = ATTN_HEAD_DIM ** -0.5 * math.log2(math.e)
    col_scale = jnp.ones((1, PROJ_W), F32).at[:, OFF_Q:OFF_Q + D_ATTN].set(q_scale)
    proj, dt_raw = _in_proj(x2, pre_norm_w.reshape(1, d), wt_main, wt_dt, col_scale)
    proj3 = proj.reshape(b, s, PROJ_W)
    dt3 = dt_raw.reshape(b, s, DT_PAD)

    pad_h = lambda v: jnp.pad(v.reshape(1, SSM_HEADS), ((0, 0), (0, DT_PAD - SSM_HEADS)))
    y_ssm = _ssd(
        proj3, dt3,
        conv_w[:, :D_SSM], conv_w[:, D_SSM:],
        conv_b[:D_SSM].reshape(1, D_SSM), conv_b[D_SSM:].reshape(1, BC_W),
        pad_h(dt_bias), pad_h(a_log),
        jnp.repeat(d_skip, SSM_HEAD_DIM).reshape(1, D_SSM),
        ssm_norm_w.reshape(1, D_SSM))

    lam_init = 0.8 - 0.6 * math.exp(-0.3 * layer)
    lamv = jnp.stack([lambda_q1, lambda_k1, lambda_q2, lambda_k2], axis=0)
    y_attn = _diff_attn(proj3, lamv, attn_subln_w.reshape(1, ATTN_VDIM), lam_init)

    out = _out_proj(y_ssm.reshape(b * s, D_SSM), y_attn.reshape(b * s, D_ATTN),
                    w_out.astype(BF16), x2, post_norm_w.reshape(1, d))
    return out.reshape(b, s, d)


def kernel(x, pre_norm_w, post_norm_w, w_in, conv_w, conv_b, dt_bias, a_log, d_skip, ssm_norm_w, lambda_q1, lambda_k1, lambda_q2, lambda_k2, attn_subln_w, w_out):
    for layer in range(w_in.shape[0]):
        x = _layer(x, layer, pre_norm_w[layer], post_norm_w[layer], w_in[layer],
                   conv_w[layer], conv_b[layer], dt_bias[layer], a_log[layer], d_skip[layer],
                   ssm_norm_w[layer], lambda_q1[layer], lambda_k1[layer], lambda_q2[layer],
                   lambda_k2[layer], attn_subln_w[layer], w_out[layer])
    return x
```

```python
import functools
import math

import jax
import jax.numpy as jnp
from jax import lax
from jax.experimental import pallas as pl
from jax.experimental.pallas import tpu as pltpu

F32 = jnp.float32
BF16 = jnp.bfloat16

D_MODEL = 4096
D_SSM = 2048
D_ATTN = 2048
SSM_HEAD_DIM = 64
SSM_HEADS = 32
SSM_GROUPS = 4
SSM_HPG = 8
SSM_STATE = 128
CONV_WIDTH = 4
CHUNK = 128
ATTN_HEAD_DIM = 64
ATTN_VDIM = 128
ATTN_HEADS = 16
NORM_EPS = 1e-6
GROUP_W = D_SSM // SSM_GROUPS
BC_W = 2 * SSM_GROUPS * SSM_STATE

LANES = 128
SUBLANES = 8
VMEM_LIMIT = 56 * 1024 * 1024

OFF_ZS = 0
OFF_ZA = 2048
OFF_XS = 4096
OFF_BC = 6144
OFF_Q = 7168
OFF_K = 9216
OFF_V = 11264
PROJ_W = 13312
W_IN_DT = OFF_Q
W_IN_QKV = W_IN_DT + SSM_HEADS
DT_PAD = LANES

IN_TM, IN_TN = 1024, 1024
IN_XR = 512
IN_NP = IN_TM // IN_XR
AT_T = 512
SSD_BPS = 2
AT_DW = 256
AT_HPS = 4
BF16_SUBLANES = 16
AT_ACC_ROWS = ATTN_VDIM + BF16_SUBLANES
OUT_TM, OUT_TN, OUT_RC = 256, 512, 64
OUT_VMEM_LIMIT = 60 * 1024 * 1024


def _silu(x):
    hx = 0.5 * x
    return hx + hx * jnp.tanh(hx)


def _dot_nt(a, b_t):
    return lax.dot_general(a, b_t, (((1,), (1,)), ((), ())), preferred_element_type=F32)


def _in_proj_kernel(x_ref, nw_ref, wt_ref, wdt_ref, cs_ref, o_ref, dt_ref, h_scr):
    s = pl.program_id(1)

    for p in range(IN_NP):
        @pl.when(s == p)
        def _(p=p):
            x = x_ref[...]
            ms = jnp.mean(x * x, axis=-1, keepdims=True)
            h = (x * lax.rsqrt(ms + NORM_EPS)) * nw_ref[...]
            hb = h.astype(BF16)
            h_scr[p * IN_XR:(p + 1) * IN_XR, :] = hb
            dt_ref[...] = _dot_nt(hb, wdt_ref[...])

    @pl.when(s >= IN_NP)
    def _():
        o_ref[...] = (_dot_nt(h_scr[...], wt_ref[...]) * cs_ref[...]).astype(o_ref.dtype)


def _in_proj(x2, norm_w, wt_main, wt_dt, col_scale):
    m = x2.shape[0]
    xrow = lambda i, s: i * IN_NP + jnp.minimum(s, IN_NP - 1)
    col = lambda s: jnp.maximum(s - IN_NP, 0)
    return pl.pallas_call(
        _in_proj_kernel,
        grid=(m // IN_TM, IN_NP + PROJ_W // IN_TN),
        in_specs=[
            pl.BlockSpec((IN_XR, D_MODEL), lambda i, s: (xrow(i, s), 0)),
            pl.BlockSpec((1, D_MODEL), lambda i, s: (0, 0)),
            pl.BlockSpec((pl.Element(IN_TN), pl.Element(D_MODEL)),
                         lambda i, s: (BF16_SUBLANES * (col(s) * (IN_TN // BF16_SUBLANES) + jnp.where(
                             col(s) * IN_TN >= W_IN_DT, SSM_HEADS // BF16_SUBLANES, 0)), 0)),
            pl.BlockSpec((DT_PAD, D_MODEL), lambda i, s: (0, 0)),
            pl.BlockSpec((1, IN_TN), lambda i, s: (0, col(s))),
        ],
        out_specs=[
            pl.BlockSpec((IN_TM, IN_TN), lambda i, s: (i, col(s))),
            pl.BlockSpec((IN_XR, DT_PAD), lambda i, s: (xrow(i, s), 0)),
        ],
        out_shape=[
            jax.ShapeDtypeStruct((m, PROJ_W), BF16),
            jax.ShapeDtypeStruct((m, DT_PAD), F32),
        ],
        scratch_shapes=[pltpu.VMEM((IN_TM, D_MODEL), BF16)],
        compiler_params=pltpu.CompilerParams(
            dimension_semantics=("parallel", "arbitrary"),
            vmem_limit_bytes=VMEM_LIMIT),
        name="in_proj",
    )(x2, norm_w, wt_main, wt_dt, col_scale)


def _split_bf16(x, terms):
    parts = []
    r = x
    for _ in range(terms):
        p = r.astype(BF16)
        parts.append(p)
        r = r - p.astype(F32)
    return parts


def _ssd_kernel(xs_ref, bc_ref, dt_ref, z_ref, cwx_ref, cwb_ref, cbx_ref, cbb_ref,
                dtb_ref, alog_ref, dsk_ref, nw_ref, y_ref, extx, extb, shift, expand, state):
    L = CHUNK
    H0 = BF16_SUBLANES

    @pl.when(pl.program_id(1) == 0)
    def _():
        extx[:, 0:H0, :] = jnp.zeros((SSD_BPS, H0, D_SSM), BF16)
        extb[:, 0:H0, :] = jnp.zeros((SSD_BPS, H0, BC_W), BF16)
        state[...] = jnp.zeros(state.shape, F32)
        out_row = lax.broadcasted_iota(jnp.int32, shift.shape, 0)
        src_row = lax.broadcasted_iota(jnp.int32, shift.shape, 1)
        want = H0 + out_row % L - (CONV_WIDTH - 1) + out_row // L
        shift[...] = jnp.where(src_row == want, 1.0, 0.0).astype(BF16)
        er = lax.broadcasted_iota(jnp.int32, expand.shape, 0)
        ec = lax.broadcasted_iota(jnp.int32, expand.shape, 1)
        expand[...] = jnp.where(ec // SSM_HEAD_DIM == er, 1.0, 0.0).astype(BF16)

    for bb in range(SSD_BPS):
        _ssd_chunk(xs_ref.at[bb], bc_ref.at[bb], dt_ref.at[bb], z_ref.at[bb], cwx_ref, cwb_ref,
                   cbx_ref, cbb_ref, dtb_ref, alog_ref, dsk_ref, nw_ref, y_ref.at[bb],
                   extx.at[bb], extb.at[bb], shift, expand, state.at[bb])


def _ssd_chunk(xs_ref, bc_ref, dt_ref, z_ref, cwx_ref, cwb_ref, cbx_ref, cbb_ref,
               dtb_ref, alog_ref, dsk_ref, nw_ref, y_ref, extx, extb, shift, expand, state):
    L = CHUNK
    H0 = BF16_SUBLANES

    extx[H0:H0 + L, :] = xs_ref[...]
    extb[H0:H0 + L, :] = bc_ref[...]

    def conv(ext, w_ref, b_ref):
        taps = jnp.dot(shift[...], ext[...], preferred_element_type=F32)
        acc = b_ref[...] + w_ref[0:1, :] * taps[0:L]
        for w in range(1, CONV_WIDTH):
            acc = acc + w_ref[w:w + 1, :] * taps[w * L:(w + 1) * L]
        return _silu(acc)

    xs = conv(extx, cwx_ref, cbx_ref)
    bc = conv(extb, cwb_ref, cbb_ref)
    extx[0:H0, :] = extx[L:L + H0, :]
    extb[0:H0, :] = extb[L:L + H0, :]

    dtr = dt_ref[...] + dtb_ref[...]
    dt = jnp.maximum(dtr, 0.0) + jnp.log1p(jnp.exp(-jnp.abs(dtr)))
    a = dt * (-jnp.exp(alog_ref[...]))

    row = lax.broadcasted_iota(jnp.int32, (L, L), 0)
    col = lax.broadcasted_iota(jnp.int32, (L, L), 1)
    tril = row >= col
    tri = jnp.where(tril, 1.0, 0.0).astype(BF16)
    a3 = jnp.concatenate(_split_bf16(a, 3), axis=1)
    cs3 = jnp.dot(tri, a3, preferred_element_type=F32)
    cs = (cs3[:, 0:LANES] + cs3[:, LANES:2 * LANES]) + cs3[:, 2 * LANES:3 * LANES]
    cs_t = cs.T

    stack = jnp.concatenate(_split_bf16(dt, 2) + _split_bf16(cs, 2), axis=0)
    ex = jnp.dot(stack, expand[...], preferred_element_type=F32)
    dt_x = ex[0:L] + ex[L:2 * L]
    cs_x = ex[2 * L:3 * L] + ex[3 * L:4 * L]
    cs_last = cs_x[L - 1:L, :]

    xdt = xs * dt_x
    xdt_b = xdt.astype(BF16)
    xdec_b = (xdt * jnp.exp(cs_last - cs_x)).astype(BF16)
    dec_out = jnp.exp(cs_x)
    dec_chunk = jnp.exp(cs_last)

    lane = lax.broadcasted_iota(jnp.int32, (L, LANES), 1)
    low_half = lane < SSM_HEAD_DIM
    neg_inf = jnp.float32(-jnp.inf)

    y_parts = []
    for g in range(SSM_GROUPS):
        gsl = slice(g * GROUP_W, (g + 1) * GROUP_W)
        b_g = bc[:, g * SSM_STATE:(g + 1) * SSM_STATE].astype(BF16)
        c_g = bc[:, BC_W // 2 + g * SSM_STATE:BC_W // 2 + (g + 1) * SSM_STATE].astype(BF16)
        cb = lax.dot_general(c_g, b_g, (((1,), (1,)), ((), ())),
                             preferred_element_type=F32)
        s_in = state[g]
        y_off = jnp.dot(c_g, s_in.astype(BF16), preferred_element_type=F32) * dec_out[:, gsl]
        s_new = lax.dot_general(b_g, xdec_b[:, gsl], (((0,), (0,)), ((), ())),
                                preferred_element_type=F32)
        state[g] = s_in * dec_chunk[:, gsl] + s_new

        pair_out = []
        for pr in range(SSM_HPG // 2):
            ms = []
            for r in (2 * pr, 2 * pr + 1):
                h = g * SSM_HPG + r
                diff = cs[:, h:h + 1] - cs_t[h:h + 1, :]
                decay = jnp.exp(jnp.where(tril, diff, neg_inf))
                ms.append((cb * decay).astype(BF16))
            m2 = jnp.concatenate(ms, axis=1)
            c0 = g * GROUP_W + pr * LANES
            xp = xdt_b[:, c0:c0 + LANES]
            zero = jnp.zeros_like(xp)
            rhs = jnp.concatenate([jnp.where(low_half, xp, zero),
                                   jnp.where(low_half, zero, xp)], axis=0)
            pair_out.append(jnp.dot(m2, rhs, preferred_element_type=F32))
        y_parts.append(jnp.concatenate(pair_out, axis=1) + y_off)
    y = jnp.concatenate(y_parts, axis=1)

    y = y + dsk_ref[...] * xs
    y = y * _silu(z_ref[...].astype(F32))
    outs = []
    for g in range(SSM_GROUPS):
        gsl = slice(g * GROUP_W, (g + 1) * GROUP_W)
        yg = y[:, gsl]
        ms = jnp.mean(yg * yg, axis=-1, keepdims=True)
        outs.append((yg * lax.rsqrt(ms + NORM_EPS)) * nw_ref[:, gsl])
    y_ref[...] = jnp.concatenate(outs, axis=1).astype(y_ref.dtype)


def _ssd(proj3, dt3, cwx, cwb, cbx, cbb, dtb, alog, dsk, nw):
    b, s, _ = proj3.shape
    nc = s // CHUNK
    full = lambda shape: pl.BlockSpec(shape, lambda bi, ci: (0, 0))
    return pl.pallas_call(
        _ssd_kernel,
        grid=(b // SSD_BPS, nc),
        in_specs=[
            pl.BlockSpec((SSD_BPS, CHUNK, D_SSM), lambda bi, ci: (bi, ci, OFF_XS // D_SSM)),
            pl.BlockSpec((SSD_BPS, CHUNK, BC_W), lambda bi, ci: (bi, ci, OFF_BC // BC_W)),
            pl.BlockSpec((SSD_BPS, CHUNK, DT_PAD), lambda bi, ci: (bi, ci, 0)),
            pl.BlockSpec((SSD_BPS, CHUNK, D_SSM), lambda bi, ci: (bi, ci, OFF_ZS // D_SSM)),
            full((CONV_WIDTH, D_SSM)), full((CONV_WIDTH, BC_W)),
            full((1, D_SSM)), full((1, BC_W)),
            full((1, DT_PAD)), full((1, DT_PAD)),
            full((1, D_SSM)), full((1, D_SSM)),
        ],
        out_specs=pl.BlockSpec((SSD_BPS, CHUNK, D_SSM), lambda bi, ci: (bi, ci, 0)),
        out_shape=jax.ShapeDtypeStruct((b, s, D_SSM), BF16),
        scratch_shapes=[
            pltpu.VMEM((SSD_BPS, BF16_SUBLANES + CHUNK, D_SSM), BF16),
            pltpu.VMEM((SSD_BPS, BF16_SUBLANES + CHUNK, BC_W), BF16),
            pltpu.VMEM((CONV_WIDTH * CHUNK, BF16_SUBLANES + CHUNK), BF16),
            pltpu.VMEM((LANES, D_SSM), BF16),
            pltpu.VMEM((SSD_BPS, SSM_GROUPS, SSM_STATE, GROUP_W), F32),
        ],
        compiler_params=pltpu.CompilerParams(
            dimension_semantics=("parallel", "arbitrary"),
            vmem_limit_bytes=VMEM_LIMIT),
        name="ssd",
    )(proj3, proj3, dt3, proj3, cwx, cwb, cbx, cbb, dtb, alog, dsk, nw)


def _attn_kernel(lam_init, q_ref, k_ref, v_ref, z_ref, lamv_ref, sw_ref, o_ref,
                 vt_scr, bias_scr, s_scr, p_scr, acc_scr):
    i = pl.program_id(2)
    tq = tk = AT_T
    nkb = vt_scr.shape[1]
    heads = range(AT_HPS)
    streams = range(2 * AT_HPS)
    hcols = lambda hh: slice(hh * ATTN_VDIM, (hh + 1) * ATTN_VDIM)

    @pl.when(i == 0)
    def _():
        kpos = lax.broadcasted_iota(jnp.int32, (tk, tq), 0)
        qpos = lax.broadcasted_iota(jnp.int32, (tk, tq), 1)
        bias_scr[...] = jnp.where(kpos <= qpos, 0.0, -jnp.inf).astype(F32)
        ones = jnp.ones((AT_ACC_ROWS - ATTN_VDIM, tk), BF16)
        for hh in heads:
            for cblk in range(nkb):
                vb = v_ref[cblk * tk:(cblk + 1) * tk, hcols(hh)].astype(F32)
                vt_scr[hh, cblk, 0:ATTN_VDIM, :] = vb.T.astype(BF16)
                vt_scr[hh, cblk, ATTN_VDIM:AT_ACC_ROWS, :] = ones

    sub = lax.broadcasted_iota(jnp.int32, (ATTN_VDIM, tq), 0)
    q_t = []
    for hh in heads:
        qh_t = q_ref[:, hcols(hh)].astype(F32).T
        q_t.append(jnp.where(sub < ATTN_HEAD_DIM, qh_t, 0.0).astype(BF16))
        q_t.append(jnp.where(sub >= ATTN_HEAD_DIM, qh_t, 0.0).astype(BF16))

    acc_scr[...] = jnp.zeros(acc_scr.shape, F32)
    neg_inf = jnp.float32(-jnp.inf)

    def scores(j, n):
        start = pl.multiple_of(j * tk, tk)
        k_b = k_ref[pl.ds(start, tk), hcols(n // 2)]
        s_scr[n] = jnp.dot(k_b, q_t[n], preferred_element_type=F32)

    def softmax(m, n, bias=None):
        s = s_scr[n]
        if bias is not None:
            s = s + bias
        m_new = jnp.maximum(m, jnp.max(s, axis=0, keepdims=True))
        p_scr[n] = jnp.exp2(s - m_new).astype(BF16)
        return m_new, jnp.exp2(m - m_new)

    def softmax_diag(n):
        m_new = []
        for c0 in range(0, tq, AT_DW):
            rows, cols = slice(0, c0 + AT_DW), slice(c0, c0 + AT_DW)
            s = s_scr[n, rows, cols] + bias_scr[rows, cols]
            m_c = jnp.max(s, axis=0, keepdims=True)
            p_scr[n, rows, cols] = jnp.exp2(s - m_c).astype(BF16)
            if c0 + AT_DW < tk:
                p_scr[n, c0 + AT_DW:tk, cols] = jnp.zeros((tk - c0 - AT_DW, AT_DW), BF16)
            m_new.append(m_c)
        return jnp.concatenate(m_new, axis=1)

    def accumulate(vblk, n, alpha):
        vt_b = vt_scr[n // 2, vblk]
        acc_scr[n] = alpha * acc_scr[n] + jnp.dot(
            vt_b, p_scr[n], preferred_element_type=F32)

    prev_block = lambda j: jnp.where(j <= 0, i, j - 1)

    def step(j, carry):
        ms, alphas = carry
        new = []
        for n in streams:
            accumulate(prev_block(j), n, alphas[n])
            new.append(softmax(ms[n], n))
            scores(j + 1, n)
        return tuple(x[0] for x in new), tuple(x[1] for x in new)

    for n in streams:
        scores(i, n)
    ms = []
    for n in streams:
        ms.append(softmax_diag(n))
        scores(0, n)
    carry = (tuple(ms), tuple(jnp.zeros((1, tq), F32) for _ in streams))
    carry = lax.fori_loop(0, jnp.maximum(i - 1, 0), step, carry)
    ms, alphas = carry
    ghost = jnp.where(i == 0, neg_inf, jnp.float32(0.0))
    last = []
    for n in streams:
        accumulate(prev_block(i - 1), n, alphas[n])
        last.append(softmax(ms[n], n, bias=ghost)[1])
    for n in streams:
        accumulate(jnp.maximum(i - 1, 0), n, last[n])

    lv = lamv_ref[...]
    lam = (jnp.exp(jnp.sum(lv[0:1] * lv[1:2], axis=-1, keepdims=True))
           - jnp.exp(jnp.sum(lv[2:3] * lv[3:4], axis=-1, keepdims=True))
           + lam_init)
    for hh in heads:
        n1, n2 = 2 * hh, 2 * hh + 1
        l1 = acc_scr[n1, ATTN_VDIM:ATTN_VDIM + 1, :]
        l2 = acc_scr[n2, ATTN_VDIM:ATTN_VDIM + 1, :]
        o_t = (acc_scr[n1, 0:ATTN_VDIM, :] * (1.0 / l1)
               - lam * (acc_scr[n2, 0:ATTN_VDIM, :] * (1.0 / l2)))
        o = o_t.T
        ms = jnp.mean(o * o, axis=-1, keepdims=True)
        on = (o * lax.rsqrt(ms + NORM_EPS)) * sw_ref[...] * (1.0 - lam_init)
        gate = _silu(z_ref[:, hcols(hh)].astype(F32))
        o_ref[:, hcols(hh)] = (on * gate).astype(o_ref.dtype)


def _diff_attn(proj3, lamv, subln_w, lam_init):
    b, s, _ = proj3.shape
    nq = s // AT_T
    gw = AT_HPS * ATTN_VDIM
    gb = lambda off: off // gw
    nstream = 2 * AT_HPS
    return pl.pallas_call(
        functools.partial(_attn_kernel, lam_init),
        grid=(b, ATTN_HEADS // AT_HPS, nq),
        in_specs=[
            pl.BlockSpec((None, AT_T, gw), lambda bi, g, i: (bi, i, gb(OFF_Q) + g)),
            pl.BlockSpec((None, s, gw), lambda bi, g, i: (bi, 0, gb(OFF_K) + g)),
            pl.BlockSpec((None, s, gw), lambda bi, g, i: (bi, 0, gb(OFF_V) + g)),
            pl.BlockSpec((None, AT_T, gw), lambda bi, g, i: (bi, i, gb(OFF_ZA) + g)),
            pl.BlockSpec((4, ATTN_HEAD_DIM), lambda bi, g, i: (0, 0)),
            pl.BlockSpec((1, ATTN_VDIM), lambda bi, g, i: (0, 0)),
        ],
        out_specs=pl.BlockSpec((None, AT_T, gw), lambda bi, g, i: (bi, i, g)),
        out_shape=jax.ShapeDtypeStruct((b, s, D_ATTN), BF16),
        scratch_shapes=[
            pltpu.VMEM((AT_HPS, s // AT_T, AT_ACC_ROWS, AT_T), BF16),
            pltpu.VMEM((AT_T, AT_T), F32),
            pltpu.VMEM((nstream, AT_T, AT_T), F32),
            pltpu.VMEM((nstream, AT_T, AT_T), BF16),
            pltpu.VMEM((nstream, AT_ACC_ROWS, AT_T), F32),
        ],
        compiler_params=pltpu.CompilerParams(
            dimension_semantics=("parallel", "parallel", "arbitrary"),
            vmem_limit_bytes=VMEM_LIMIT),
        name="diffattn",
    )(proj3, proj3, proj3, proj3, lamv, subln_w)


def _out_proj_kernel(ys_ref, ya_ref, w_ref, x_ref, nw_ref, o_ref):
    ys = ys_ref[...]
    ya = ya_ref[...]
    ssq = jnp.zeros((OUT_TM, 1), F32)
    for jj in range(D_MODEL // OUT_TN):
        sl = slice(jj * OUT_TN, (jj + 1) * OUT_TN)
        acc = (jnp.dot(ys, w_ref[0:D_SSM, sl], preferred_element_type=F32)
               + jnp.dot(ya, w_ref[D_SSM:D_SSM + D_ATTN, sl], preferred_element_type=F32))
        ssq = ssq + jnp.sum(acc * acc, axis=-1, keepdims=True)
        o_ref[:, sl] = acc
    inv = lax.rsqrt(ssq * (1.0 / D_MODEL) + NORM_EPS)
    for r in range(OUT_TM // OUT_RC):
        rows = slice(r * OUT_RC, (r + 1) * OUT_RC)
        inv_r = inv[rows]
        for jj in range(D_MODEL // OUT_TN):
            sl = slice(jj * OUT_TN, (jj + 1) * OUT_TN)
            o_ref[rows, sl] = x_ref[rows, sl] + (o_ref[rows, sl] * inv_r) * nw_ref[:, sl]


def _out_proj(ys, ya, w_o, x2, norm_w):
    m = x2.shape[0]
    return pl.pallas_call(
        _out_proj_kernel,
        grid=(m // OUT_TM,),
        in_specs=[
            pl.BlockSpec((OUT_TM, D_SSM), lambda i: (i, 0)),
            pl.BlockSpec((OUT_TM, D_ATTN), lambda i: (i, 0)),
            pl.BlockSpec((D_SSM + D_ATTN, D_MODEL), lambda i: (0, 0),
                         pipeline_mode=pl.Buffered(1)),
            pl.BlockSpec((OUT_TM, D_MODEL), lambda i: (i, 0)),
            pl.BlockSpec((1, D_MODEL), lambda i: (0, 0)),
        ],
        out_specs=pl.BlockSpec((OUT_TM, D_MODEL), lambda i: (i, 0)),
        out_shape=jax.ShapeDtypeStruct((m, D_MODEL), F32),
        compiler_params=pltpu.CompilerParams(
            dimension_semantics=("parallel",),
            vmem_limit_bytes=OUT_VMEM_LIMIT),
        name="out_proj",
    )(ys, ya, w_o, x2, norm_w)


def _layer(x, layer, pre_norm_w, post_norm_w, w_in, conv_w, conv_b, dt_bias, a_log, d_skip,
           ssm_norm_w, lambda_q1, lambda_k1, lambda_q2, lambda_k2, attn_subln_w, w_out):
    b, s, d = x.shape
    x2 = x.reshape(b * s, d)

    wt_main = w_in.T.astype(BF16)
    wt_dt = jnp.pad(wt_main[W_IN_DT:W_IN_QKV], ((0, DT_PAD - SSM_HEADS), (0, 0)))

    q_scale = ATTN_HEAD_DIM ** -0.5 * math.log2(math.e)
    col_scale = jnp.ones((1, PROJ_W), F32).at[:, OFF_Q:OFF_Q + D_ATTN].set(q_scale)
    proj, dt_raw = _in_proj(x2, pre_norm_w.reshape(1, d), wt_main, wt_dt, col_scale)
    proj3 = proj.reshape(b, s, PROJ_W)
    dt3 = dt_raw.reshape(b, s, DT_PAD)

    pad_h = lambda v: jnp.pad(v.reshape(1, SSM_HEADS), ((0, 0), (0, DT_PAD - SSM_HEADS)))
    y_ssm = _ssd(
        proj3, dt3,
        conv_w[:, :D_SSM], conv_w[:, D_SSM:],
        conv_b[:D_SSM].reshape(1, D_SSM), conv_b[D_SSM:].reshape(1, BC_W),
        pad_h(dt_bias), pad_h(a_log),
        jnp.repeat(d_skip, SSM_HEAD_DIM).reshape(1, D_SSM),
        ssm_norm_w.reshape(1, D_SSM))

    lam_init = 0.8 - 0.6 * math.exp(-0.3 * layer)
    lamv = jnp.stack([lambda_q1, lambda_k1, lambda_q2, lambda_k2], axis=0)
    y_attn = _diff_attn(proj3, lamv, attn_subln_w.reshape(1, ATTN_VDIM), lam_init)

    out = _out_proj(y_ssm.reshape(b * s, D_SSM), y_attn.reshape(b * s, D_ATTN),
                    w_out.astype(BF16), x2, post_norm_w.reshape(1, d))
    return out.reshape(b, s, d)


def kernel(x, pre_norm_w, post_norm_w, w_in, conv_w, conv_b, dt_bias, a_log, d_skip, ssm_norm_w, lambda_q1, lambda_k1, lambda_q2, lambda_k2, attn_subln_w, w_out):
    for layer in range(w_in.shape[0]):
        x = _layer(x, layer, pre_norm_w[layer], post_norm_w[layer], w_in[layer],
                   conv_w[layer], conv_b[layer], dt_bias[layer], a_log[layer], d_skip[layer],
                   ssm_norm_w[layer], lambda_q1[layer], lambda_k1[layer], lambda_q2[layer],
                   lambda_k2[layer], attn_subln_w[layer], w_out[layer])
    return x
```

```python
import functools
import math

import jax
import jax.numpy as jnp
from jax import lax
from jax.experimental import pallas as pl
from jax.experimental.pallas import tpu as pltpu

F32 = jnp.float32
BF16 = jnp.bfloat16

D_MODEL = 4096
D_SSM = 2048
D_ATTN = 2048
SSM_HEAD_DIM = 64
SSM_HEADS = 32
SSM_GROUPS = 4
SSM_HPG = 8
SSM_STATE = 128
CONV_WIDTH = 4
CHUNK = 128
ATTN_HEAD_DIM = 64
ATTN_VDIM = 128
ATTN_HEADS = 16
NORM_EPS = 1e-6
GROUP_W = D_SSM // SSM_GROUPS
BC_W = 2 * SSM_GROUPS * SSM_STATE

LANES = 128
SUBLANES = 8
VMEM_LIMIT = 56 * 1024 * 1024

OFF_ZS = 0
OFF_ZA = 2048
OFF_XS = 4096
OFF_BC = 6144
OFF_Q = 7168
OFF_K = 9216
OFF_V = 11264
PROJ_W = 13312
W_IN_DT = OFF_Q
W_IN_QKV = W_IN_DT + SSM_HEADS
DT_PAD = LANES

IN_TM, IN_TN = 1024, 1024
IN_XR = 512
IN_NP = IN_TM // IN_XR
AT_T = 512
SSD_BPS = 2
AT_DW = 256
AT_HPS = 4
BF16_SUBLANES = 16
AT_ACC_ROWS = ATTN_VDIM + BF16_SUBLANES
OUT_TM, OUT_TN, OUT_RC = 256, 512, 64
OUT_VMEM_LIMIT = 60 * 1024 * 1024


def _silu(x):
    hx = 0.5 * x
    return hx + hx * jnp.tanh(hx)


def _dot_nt(a, b_t):
    return lax.dot_general(a, b_t, (((1,), (1,)), ((), ())), preferred_element_type=F32)


def _in_proj_kernel(x_ref, nw_ref, wt_ref, wdt_ref, cs_ref, o_ref, dt_ref, h_scr):
    s = pl.program_id(1)

    for p in range(IN_NP):
        @pl.when(s == p)
        def _(p=p):
            x = x_ref[...]
            ms = jnp.mean(x * x, axis=-1, keepdims=True)
            h = (x * lax.rsqrt(ms + NORM_EPS)) * nw_ref[...]
            hb = h.astype(BF16)
            h_scr[p * IN_XR:(p + 1) * IN_XR, :] = hb
            dt_ref[...] = _dot_nt(hb, wdt_ref[...])

    @pl.when(s >= IN_NP)
    def _():
        o_ref[...] = (_dot_nt(h_scr[...], wt_ref[...]) * cs_ref[...]).astype(o_ref.dtype)


def _in_proj(x2, norm_w, wt_main, wt_dt, col_scale):
    m = x2.shape[0]
    xrow = lambda i, s: i * IN_NP + jnp.minimum(s, IN_NP - 1)
    col = lambda s: jnp.maximum(s - IN_NP, 0)
    return pl.pallas_call(
        _in_proj_kernel,
        grid=(m // IN_TM, IN_NP + PROJ_W // IN_TN),
        in_specs=[
            pl.BlockSpec((IN_XR, D_MODEL), lambda i, s: (xrow(i, s), 0)),
            pl.BlockSpec((1, D_MODEL), lambda i, s: (0, 0)),
            pl.BlockSpec((pl.Element(IN_TN), pl.Element(D_MODEL)),
                         lambda i, s: (BF16_SUBLANES * (col(s) * (IN_TN // BF16_SUBLANES) + jnp.where(
                             col(s) * IN_TN >= W_IN_DT, SSM_HEADS // BF16_SUBLANES, 0)), 0)),
            pl.BlockSpec((DT_PAD, D_MODEL), lambda i, s: (0, 0)),
            pl.BlockSpec((1, IN_TN), lambda i, s: (0, col(s))),
        ],
        out_specs=[
            pl.BlockSpec((IN_TM, IN_TN), lambda i, s: (i, col(s))),
            pl.BlockSpec((IN_XR, DT_PAD), lambda i, s: (xrow(i, s), 0)),
        ],
        out_shape=[
            jax.ShapeDtypeStruct((m, PROJ_W), BF16),
            jax.ShapeDtypeStruct((m, DT_PAD), F32),
        ],
        scratch_shapes=[pltpu.VMEM((IN_TM, D_MODEL), BF16)],
        compiler_params=pltpu.CompilerParams(
            dimension_semantics=("parallel", "arbitrary"),
            vmem_limit_bytes=VMEM_LIMIT),
        name="in_proj",
    )(x2, norm_w, wt_main, wt_dt, col_scale)


def _split_bf16(x, terms):
    parts = []
    r = x
    for _ in range(terms):
        p = r.astype(BF16)
        parts.append(p)
        r = r - p.astype(F32)
    return parts


def _ssd_kernel(xs_ref, bc_ref, dt_ref, z_ref, cwx_ref, cwb_ref, cbx_ref, cbb_ref,
                dtb_ref, alog_ref, dsk_ref, nw_ref, y_ref, extx, extb, shift, expand, state):
    L = CHUNK
    H0 = BF16_SUBLANES

    @pl.when(pl.program_id(1) == 0)
    def _():
        extx[:, 0:H0, :] = jnp.zeros((SSD_BPS, H0, D_SSM), BF16)
        extb[:, 0:H0, :] = jnp.zeros((SSD_BPS, H0, BC_W), BF16)
        state[...] = jnp.zeros(state.shape, F32)
        out_row = lax.broadcasted_iota(jnp.int32, shift.shape, 0)
        src_row = lax.broadcasted_iota(jnp.int32, shift.shape, 1)
        want = H0 + out_row % L - (CONV_WIDTH - 1) + out_row // L
        shift[...] = jnp.where(src_row == want, 1.0, 0.0).astype(BF16)
        er = lax.broadcasted_iota(jnp.int32, expand.shape, 0)
        ec = lax.broadcasted_iota(jnp.int32, expand.shape, 1)
        expand[...] = jnp.where(ec // SSM_HEAD_DIM == er, 1.0, 0.0).astype(BF16)

    for bb in range(SSD_BPS):
        _ssd_chunk(xs_ref.at[bb], bc_ref.at[bb], dt_ref.at[bb], z_ref.at[bb], cwx_ref, cwb_ref,
                   cbx_ref, cbb_ref, dtb_ref, alog_ref, dsk_ref, nw_ref, y_ref.at[bb],
                   extx.at[bb], extb.at[bb], shift, expand, state.at[bb])


def _ssd_chunk(xs_ref, bc_ref, dt_ref, z_ref, cwx_ref, cwb_ref, cbx_ref, cbb_ref,
               dtb_ref, alog_ref, dsk_ref, nw_ref, y_ref, extx, extb, shift, expand, state):
    L = CHUNK
    H0 = BF16_SUBLANES

    extx[H0:H0 + L, :] = xs_ref[...]
    extb[H0:H0 + L, :] = bc_ref[...]

    def conv(ext, w_ref, b_ref):
        taps = jnp.dot(shift[...], ext[...], preferred_element_type=F32)
        acc = b_ref[...] + w_ref[0:1, :] * taps[0:L]
        for w in range(1, CONV_WIDTH):
            acc = acc + w_ref[w:w + 1, :] * taps[w * L:(w + 1) * L]
        return _silu(acc)

    xs = conv(extx, cwx_ref, cbx_ref)
    bc = conv(extb, cwb_ref, cbb_ref)
    extx[0:H0, :] = extx[L:L + H0, :]
    extb[0:H0, :] = extb[L:L + H0, :]

    dtr = dt_ref[...] + dtb_ref[...]
    dt = jnp.maximum(dtr, 0.0) + jnp.log1p(jnp.exp(-jnp.abs(dtr)))
    a = dt * (-jnp.exp(alog_ref[...]))

    row = lax.broadcasted_iota(jnp.int32, (L, L), 0)
    col = lax.broadcasted_iota(jnp.int32, (L, L), 1)
    tril = row >= col
    tri = jnp.where(tril, 1.0, 0.0).astype(BF16)
    a3 = jnp.concatenate(_split_bf16(a, 3), axis=1)
    cs3 = jnp.dot(tri, a3, preferred_element_type=F32)
    cs = (cs3[:, 0:LANES] + cs3[:, LANES:2 * LANES]) + cs3[:, 2 * LANES:3 * LANES]
    cs_t = cs.T

    stack = jnp.concatenate(_split_bf16(dt, 2) + _split_bf16(cs, 2), axis=0)
    ex = jnp.dot(stack, expand[...], preferred_element_type=F32)
    dt_x = ex[0:L] + ex[L:2 * L]
    cs_x = ex[2 * L:3 * L] + ex[3 * L:4 * L]
    cs_last = cs_x[L - 1:L, :]

    xdt = xs * dt_x
    xdt_b = xdt.astype(BF16)
    xdec_b = (xdt * jnp.exp(cs_last - cs_x)).astype(BF16)
    dec_out = jnp.exp(cs_x)
    dec_chunk = jnp.exp(cs_last)

    lane = lax.broadcasted_iota(jnp.int32, (L, LANES), 1)
    low_half = lane < SSM_HEAD_DIM
    neg_inf = jnp.float32(-jnp.inf)

    y_parts = []
    for g in range(SSM_GROUPS):
        gsl = slice(g * GROUP_W, (g + 1) * GROUP_W)
        b_g = bc[:, g * SSM_STATE:(g + 1) * SSM_STATE].astype(BF16)
        c_g = bc[:, BC_W // 2 + g * SSM_STATE:BC_W // 2 + (g + 1) * SSM_STATE].astype(BF16)
        cb = lax.dot_general(c_g, b_g, (((1,), (1,)), ((), ())),
                             preferred_element_type=F32)
        s_in = state[g]
        y_off = jnp.dot(c_g, s_in.astype(BF16), preferred_element_type=F32) * dec_out[:, gsl]
        s_new = lax.dot_general(b_g, xdec_b[:, gsl], (((0,), (0,)), ((), ())),
                                preferred_element_type=F32)
        state[g] = s_in * dec_chunk[:, gsl] + s_new

        pair_out = []
        for pr in range(SSM_HPG // 2):
            ms = []
            for r in (2 * pr, 2 * pr + 1):
                h = g * SSM_HPG + r
                diff = cs[:, h:h + 1] - cs_t[h:h + 1, :]
                decay = jnp.exp(jnp.where(tril, diff, neg_inf))
                ms.append((cb * decay).astype(BF16))
            m2 = jnp.concatenate(ms, axis=1)
            c0 = g * GROUP_W + pr * LANES
            xp = xdt_b[:, c0:c0 + LANES]
            zero = jnp.zeros_like(xp)
            rhs = jnp.concatenate([jnp.where(low_half, xp, zero),
                                   jnp.where(low_half, zero, xp)], axis=0)
            pair_out.append(jnp.dot(m2, rhs, preferred_element_type=F32))
        y_parts.append(jnp.concatenate(pair_out, axis=1) + y_off)
    y = jnp.concatenate(y_parts, axis=1)

    y = y + dsk_ref[...] * xs
    y = y * _silu(z_ref[...].astype(F32))
    outs = []
    for g in range(SSM_GROUPS):
        gsl = slice(g * GROUP_W, (g + 1) * GROUP_W)
        yg = y[:, gsl]
        ms = jnp.mean(yg * yg, axis=-1, keepdims=True)
        outs.append((yg * lax.rsqrt(ms + NORM_EPS)) * nw_ref[:, gsl])
    y_ref[...] = jnp.concatenate(outs, axis=1).astype(y_ref.dtype)


def _ssd(proj3, dt3, cwx, cwb, cbx, cbb, dtb, alog, dsk, nw):
    b, s, _ = proj3.shape
    nc = s // CHUNK
    full = lambda shape: pl.BlockSpec(shape, lambda bi, ci: (0, 0))
    return pl.pallas_call(
        _ssd_kernel,
        grid=(b // SSD_BPS, nc),
        in_specs=[
            pl.BlockSpec((SSD_BPS, CHUNK, D_SSM), lambda bi, ci: (bi, ci, OFF_XS // D_SSM)),
            pl.BlockSpec((SSD_BPS, CHUNK, BC_W), lambda bi, ci: (bi, ci, OFF_BC // BC_W)),
            pl.BlockSpec((SSD_BPS, CHUNK, DT_PAD), lambda bi, ci: (bi, ci, 0)),
            pl.BlockSpec((SSD_BPS, CHUNK, D_SSM), lambda bi, ci: (bi, ci, OFF_ZS // D_SSM)),
            full((CONV_WIDTH, D_SSM)), full((CONV_WIDTH, BC_W)),
            full((1, D_SSM)), full((1, BC_W)),
            full((1, DT_PAD)), full((1, DT_PAD)),
            full((1, D_SSM)), full((1, D_SSM)),
        ],
        out_specs=pl.BlockSpec((SSD_BPS, CHUNK, D_SSM), lambda bi, ci: (bi, ci, 0)),
        out_shape=jax.ShapeDtypeStruct((b, s, D_SSM), BF16),
        scratch_shapes=[
            pltpu.VMEM((SSD_BPS, BF16_SUBLANES + CHUNK, D_SSM), BF16),
            pltpu.VMEM((SSD_BPS, BF16_SUBLANES + CHUNK, BC_W), BF16),
            pltpu.VMEM((CONV_WIDTH * CHUNK, BF16_SUBLANES + CHUNK), BF16),
            pltpu.VMEM((LANES, D_SSM), BF16),
            pltpu.VMEM((SSD_BPS, SSM_GROUPS, SSM_STATE, GROUP_W), F32),
        ],
        compiler_params=pltpu.CompilerParams(
            dimension_semantics=("parallel", "arbitrary"),
            vmem_limit_bytes=VMEM_LIMIT),
        name="ssd",
    )(proj3, proj3, dt3, proj3, cwx, cwb, cbx, cbb, dtb, alog, dsk, nw)


def _attn_kernel(lam_init, q_ref, k_ref, v_ref, z_ref, lamv_ref, sw_ref, o_ref,
                 vt_scr, bias_scr, s_scr, bm_scr, p_scr, acc_scr):
    i = pl.program_id(2)
    tq = tk = AT_T
    nkb = vt_scr.shape[1]
    heads = range(AT_HPS)
    streams = range(2 * AT_HPS)
    hcols = lambda hh: slice(hh * ATTN_VDIM, (hh + 1) * ATTN_VDIM)

    @pl.when(i == 0)
    def _():
        kpos = lax.broadcasted_iota(jnp.int32, (tk, tq), 0)
        qpos = lax.broadcasted_iota(jnp.int32, (tk, tq), 1)
        bias_scr[...] = jnp.where(kpos <= qpos, 0.0, -jnp.inf).astype(F32)
        ones = jnp.ones((AT_ACC_ROWS - ATTN_VDIM, tk), BF16)
        for hh in heads:
            for cblk in range(nkb):
                vb = v_ref[cblk * tk:(cblk + 1) * tk, hcols(hh)].astype(F32)
                vt_scr[hh, cblk, 0:ATTN_VDIM, :] = vb.T.astype(BF16)
                vt_scr[hh, cblk, ATTN_VDIM:AT_ACC_ROWS, :] = ones

    sub = lax.broadcasted_iota(jnp.int32, (ATTN_VDIM, tq), 0)
    q_t = []
    for hh in heads:
        qh_t = q_ref[:, hcols(hh)].astype(F32).T
        q_t.append(jnp.where(sub < ATTN_HEAD_DIM, qh_t, 0.0).astype(BF16))
        q_t.append(jnp.where(sub >= ATTN_HEAD_DIM, qh_t, 0.0).astype(BF16))

    acc_scr[...] = jnp.zeros(acc_scr.shape, F32)
    neg_inf = jnp.float32(-jnp.inf)

    def scores(j, n):
        start = pl.multiple_of(j * tk, tk)
        k_b = k_ref[pl.ds(start, tk), hcols(n // 2)]
        s_scr[n] = jnp.dot(k_b, q_t[n], preferred_element_type=F32)
        bm_scr[n] = jnp.max(s_scr[n].reshape(tk // SUBLANES, SUBLANES, tq), axis=0)

    def softmax(m, n, bias=None):
        s, bm = s_scr[n], jnp.max(bm_scr[n], axis=0, keepdims=True)
        if bias is not None:
            s, bm = s + bias, bm + bias
        m_new = jnp.maximum(m, bm)
        p_scr[n] = jnp.exp2(s - m_new).astype(BF16)
        return m_new, jnp.exp2(m - m_new)

    def softmax_diag(n):
        m_new = []
        for c0 in range(0, tq, AT_DW):
            rows, cols = slice(0, c0 + AT_DW), slice(c0, c0 + AT_DW)
            s = s_scr[n, rows, cols] + bias_scr[rows, cols]
            m_c = jnp.max(s, axis=0, keepdims=True)
            p_scr[n, rows, cols] = jnp.exp2(s - m_c).astype(BF16)
            if c0 + AT_DW < tk:
                p_scr[n, c0 + AT_DW:tk, cols] = jnp.zeros((tk - c0 - AT_DW, AT_DW), BF16)
            m_new.append(m_c)
        return jnp.concatenate(m_new, axis=1)

    def accumulate(vblk, n, alpha):
        vt_b = vt_scr[n // 2, vblk]
        acc_scr[n] = alpha * acc_scr[n] + jnp.dot(
            vt_b, p_scr[n], preferred_element_type=F32)

    prev_block = lambda j: jnp.where(j <= 0, i, j - 1)

    def step(j, carry):
        ms, alphas = carry
        new = []
        for n in streams:
            accumulate(prev_block(j), n, alphas[n])
            new.append(softmax(ms[n], n))
            scores(j + 1, n)
        return tuple(x[0] for x in new), tuple(x[1] for x in new)

    for n in streams:
        scores(i, n)
    ms = []
    for n in streams:
        ms.append(softmax_diag(n))
        scores(0, n)
    carry = (tuple(ms), tuple(jnp.zeros((1, tq), F32) for _ in streams))
    carry = lax.fori_loop(0, jnp.maximum(i - 1, 0), step, carry)
    ms, alphas = carry
    ghost = jnp.where(i == 0, neg_inf, jnp.float32(0.0))
    last = []
    for n in streams:
        accumulate(prev_block(i - 1), n, alphas[n])
        last.append(softmax(ms[n], n, bias=ghost)[1])
    for n in streams:
        accumulate(jnp.maximum(i - 1, 0), n, last[n])

    lv = lamv_ref[...]
    lam = (jnp.exp(jnp.sum(lv[0:1] * lv[1:2], axis=-1, keepdims=True))
           - jnp.exp(jnp.sum(lv[2:3] * lv[3:4], axis=-1, keepdims=True))
           + lam_init)
    for hh in heads:
        n1, n2 = 2 * hh, 2 * hh + 1
        l1 = acc_scr[n1, ATTN_VDIM:ATTN_VDIM + 1, :]
        l2 = acc_scr[n2, ATTN_VDIM:ATTN_VDIM + 1, :]
        o_t = (acc_scr[n1, 0:ATTN_VDIM, :] * (1.0 / l1)
               - lam * (acc_scr[n2, 0:ATTN_VDIM, :] * (1.0 / l2)))
        o = o_t.T
        ms = jnp.mean(o * o, axis=-1, keepdims=True)
        on = (o * lax.rsqrt(ms + NORM_EPS)) * sw_ref[...] * (1.0 - lam_init)
        gate = _silu(z_ref[:, hcols(hh)].astype(F32))
        o_ref[:, hcols(hh)] = (on * gate).astype(o_ref.dtype)


def _diff_attn(proj3, lamv, subln_w, lam_init):
    b, s, _ = proj3.shape
    nq = s // AT_T
    gw = AT_HPS * ATTN_VDIM
    gb = lambda off: off // gw
    nstream = 2 * AT_HPS
    return pl.pallas_call(
        functools.partial(_attn_kernel, lam_init),
        grid=(b, ATTN_HEADS // AT_HPS, nq),
        in_specs=[
            pl.BlockSpec((None, AT_T, gw), lambda bi, g, i: (bi, i, gb(OFF_Q) + g)),
            pl.BlockSpec((None, s, gw), lambda bi, g, i: (bi, 0, gb(OFF_K) + g)),
            pl.BlockSpec((None, s, gw), lambda bi, g, i: (bi, 0, gb(OFF_V) + g)),
            pl.BlockSpec((None, AT_T, gw), lambda bi, g, i: (bi, i, gb(OFF_ZA) + g)),
            pl.BlockSpec((4, ATTN_HEAD_DIM), lambda bi, g, i: (0, 0)),
            pl.BlockSpec((1, ATTN_VDIM), lambda bi, g, i: (0, 0)),
        ],
        out_specs=pl.BlockSpec((None, AT_T, gw), lambda bi, g, i: (bi, i, g)),
        out_shape=jax.ShapeDtypeStruct((b, s, D_ATTN), BF16),
        scratch_shapes=[
            pltpu.VMEM((AT_HPS, s // AT_T, AT_ACC_ROWS, AT_T), BF16),
            pltpu.VMEM((AT_T, AT_T), F32),
            pltpu.VMEM((nstream, AT_T, AT_T), F32),
            pltpu.VMEM((nstream, SUBLANES, AT_T), F32),
            pltpu.VMEM((nstream, AT_T, AT_T), BF16),
            pltpu.VMEM((nstream, AT_ACC_ROWS, AT_T), F32),
        ],
        compiler_params=pltpu.CompilerParams(
            dimension_semantics=("parallel", "parallel", "arbitrary"),
            vmem_limit_bytes=VMEM_LIMIT),
        name="diffattn",
    )(proj3, proj3, proj3, proj3, lamv, subln_w)


def _out_proj_kernel(ys_ref, ya_ref, w_ref, x_ref, nw_ref, o_ref):
    ys = ys_ref[...]
    ya = ya_ref[...]
    ssq = jnp.zeros((OUT_TM, 1), F32)
    for jj in range(D_MODEL // OUT_TN):
        sl = slice(jj * OUT_TN, (jj + 1) * OUT_TN)
        acc = (jnp.dot(ys, w_ref[0:D_SSM, sl], preferred_element_type=F32)
               + jnp.dot(ya, w_ref[D_SSM:D_SSM + D_ATTN, sl], preferred_element_type=F32))
        ssq = ssq + jnp.sum(acc * acc, axis=-1, keepdims=True)
        o_ref[:, sl] = acc
    inv = lax.rsqrt(ssq * (1.0 / D_MODEL) + NORM_EPS)
    for r in range(OUT_TM // OUT_RC):
        rows = slice(r * OUT_RC, (r + 1) * OUT_RC)
        inv_r = inv[rows]
        for jj in range(D_MODEL // OUT_TN):
            sl = slice(jj * OUT_TN, (jj + 1) * OUT_TN)
            o_ref[rows, sl] = x_ref[rows, sl] + (o_ref[rows, sl] * inv_r) * nw_ref[:, sl]


def _out_proj(ys, ya, w_o, x2, norm_w):
    m = x2.shape[0]
    return pl.pallas_call(
        _out_proj_kernel,
        grid=(m // OUT_TM,),
        in_specs=[
            pl.BlockSpec((OUT_TM, D_SSM), lambda i: (i, 0)),
            pl.BlockSpec((OUT_TM, D_ATTN), lambda i: (i, 0)),
            pl.BlockSpec((D_SSM + D_ATTN, D_MODEL), lambda i: (0, 0),
                         pipeline_mode=pl.Buffered(1)),
            pl.BlockSpec((OUT_TM, D_MODEL), lambda i: (i, 0)),
            pl.BlockSpec((1, D_MODEL), lambda i: (0, 0)),
        ],
        out_specs=pl.BlockSpec((OUT_TM, D_MODEL), lambda i: (i, 0)),
        out_shape=jax.ShapeDtypeStruct((m, D_MODEL), F32),
        compiler_params=pltpu.CompilerParams(
            dimension_semantics=("parallel",),
            vmem_limit_bytes=OUT_VMEM_LIMIT),
        name="out_proj",
    )(ys, ya, w_o, x2, norm_w)


def _layer(x, layer, pre_norm_w, post_norm_w, w_in, conv_w, conv_b, dt_bias, a_log, d_skip,
           ssm_norm_w, lambda_q1, lambda_k1, lambda_q2, lambda_k2, attn_subln_w, w_out):
    b, s, d = x.shape
    x2 = x.reshape(b * s, d)

    wt_main = w_in.T.astype(BF16)
    wt_dt = jnp.pad(wt_main[W_IN_DT:W_IN_QKV], ((0, DT_PAD - SSM_HEADS), (0, 0)))

    q_scale = ATTN_HEAD_DIM ** -0.5 * math.log2(math.e)
    col_scale = jnp.ones((1, PROJ_W), F32).at[:, OFF_Q:OFF_Q + D_ATTN].set(q_scale)
    proj, dt_raw = _in_proj(x2, pre_norm_w.reshape(1, d), wt_main, wt_dt, col_scale)
    proj3 = proj.reshape(b, s, PROJ_W)
    dt3 = dt_raw.reshape(b, s, DT_PAD)

    pad_h = lambda v: jnp.pad(v.reshape(1, SSM_HEADS), ((0, 0), (0, DT_PAD - SSM_HEADS)))
    y_ssm = _ssd(
        proj3, dt3,
        conv_w[:, :D_SSM], conv_w[:, D_SSM:],
        conv_b[:D_SSM].reshape(1, D_SSM), conv_b[D_SSM:].reshape(1, BC_W),
        pad_h(dt_bias), pad_h(a_log),
        jnp.repeat(d_skip, SSM_HEAD_DIM).reshape(1, D_SSM),
        ssm_norm_w.reshape(1, D_SSM))

    lam_init = 0.8 - 0.6 * math.exp(-0.3 * layer)
    lamv = jnp.stack([lambda_q1, lambda_k1, lambda_q2, lambda_k2], axis=0)
    y_attn = _diff_attn(proj3, lamv, attn_subln_w.reshape(1, ATTN_VDIM), lam_init)

    out = _out_proj(y_ssm.reshape(b * s, D_SSM), y_attn.reshape(b * s, D_ATTN),
                    w_out.astype(BF16), x2, post_norm_w.reshape(1, d))
    return out.reshape(b, s, d)


def kernel(x, pre_norm_w, post_norm_w, w_in, conv_w, conv_b, dt_bias, a_log, d_skip, ssm_norm_w, lambda_q1, lambda_k1, lambda_q2, lambda_k2, attn_subln_w, w_out):
    for layer in range(w_in.shape[0]):
        x = _layer(x, layer, pre_norm_w[layer], post_norm_w[layer], w_in[layer],
                   conv_w[layer], conv_b[layer], dt_bias[layer], a_log[layer], d_skip[layer],
                   ssm_norm_w[layer], lambda_q1[layer], lambda_k1[layer], lambda_q2[layer],
                   lambda_k2[layer], attn_subln_w[layer], w_out[layer])
    return x
```

```python
import functools
import math

import jax
import jax.numpy as jnp
from jax import lax
from jax.experimental import pallas as pl
from jax.experimental.pallas import tpu as pltpu

F32 = jnp.float32
BF16 = jnp.bfloat16

D_MODEL = 4096
D_SSM = 2048
D_ATTN = 2048
SSM_HEAD_DIM = 64
SSM_HEADS = 32
SSM_GROUPS = 4
SSM_HPG = 8
SSM_STATE = 128
CONV_WIDTH = 4
CHUNK = 128
ATTN_HEAD_DIM = 64
ATTN_VDIM = 128
ATTN_HEADS = 16
NORM_EPS = 1e-6
GROUP_W = D_SSM // SSM_GROUPS
BC_W = 2 * SSM_GROUPS * SSM_STATE

LANES = 128
SUBLANES = 8
VMEM_LIMIT = 56 * 1024 * 1024

OFF_ZS = 0
OFF_ZA = 2048
OFF_XS = 4096
OFF_BC = 6144
OFF_Q = 7168
OFF_K = 9216
OFF_V = 11264
PROJ_W = 13312
W_IN_DT = OFF_Q
W_IN_QKV = W_IN_DT + SSM_HEADS
DT_PAD = LANES

IN_TM, IN_TN = 1024, 1024
IN_XR = 512
IN_NP = IN_TM // IN_XR
AT_T = 512
SSD_BPS = 2
AT_HPS = 4
BF16_SUBLANES = 16
AT_ACC_ROWS = ATTN_VDIM + BF16_SUBLANES
OUT_TM, OUT_TN, OUT_RC = 256, 512, 64
OUT_VMEM_LIMIT = 60 * 1024 * 1024


def _silu(x):
    hx = 0.5 * x
    return hx + hx * jnp.tanh(hx)


def _dot_nt(a, b_t):
    return lax.dot_general(a, b_t, (((1,), (1,)), ((), ())), preferred_element_type=F32)


def _in_proj_kernel(x_ref, nw_ref, wt_ref, wdt_ref, cs_ref, o_ref, dt_ref, h_scr):
    s = pl.program_id(1)

    for p in range(IN_NP):
        @pl.when(s == p)
        def _(p=p):
            x = x_ref[...]
            ms = jnp.mean(x * x, axis=-1, keepdims=True)
            h = (x * lax.rsqrt(ms + NORM_EPS)) * nw_ref[...]
            hb = h.astype(BF16)
            h_scr[p * IN_XR:(p + 1) * IN_XR, :] = hb
            dt_ref[...] = _dot_nt(hb, wdt_ref[...])

    @pl.when(s >= IN_NP)
    def _():
        o_ref[...] = (_dot_nt(h_scr[...], wt_ref[...]) * cs_ref[...]).astype(o_ref.dtype)


def _in_proj(x2, norm_w, wt_main, wt_dt, col_scale):
    m = x2.shape[0]
    xrow = lambda i, s: i * IN_NP + jnp.minimum(s, IN_NP - 1)
    col = lambda s: jnp.maximum(s - IN_NP, 0)
    return pl.pallas_call(
        _in_proj_kernel,
        grid=(m // IN_TM, IN_NP + PROJ_W // IN_TN),
        in_specs=[
            pl.BlockSpec((IN_XR, D_MODEL), lambda i, s: (xrow(i, s), 0)),
            pl.BlockSpec((1, D_MODEL), lambda i, s: (0, 0)),
            pl.BlockSpec((pl.Element(IN_TN), pl.Element(D_MODEL)),
                         lambda i, s: (BF16_SUBLANES * (col(s) * (IN_TN // BF16_SUBLANES) + jnp.where(
                             col(s) * IN_TN >= W_IN_DT, SSM_HEADS // BF16_SUBLANES, 0)), 0)),
            pl.BlockSpec((DT_PAD, D_MODEL), lambda i, s: (0, 0)),
            pl.BlockSpec((1, IN_TN), lambda i, s: (0, col(s))),
        ],
        out_specs=[
            pl.BlockSpec((IN_TM, IN_TN), lambda i, s: (i, col(s))),
            pl.BlockSpec((IN_XR, DT_PAD), lambda i, s: (xrow(i, s), 0)),
        ],
        out_shape=[
            jax.ShapeDtypeStruct((m, PROJ_W), BF16),
            jax.ShapeDtypeStruct((m, DT_PAD), F32),
        ],
        scratch_shapes=[pltpu.VMEM((IN_TM, D_MODEL), BF16)],
        compiler_params=pltpu.CompilerParams(
            dimension_semantics=("parallel", "arbitrary"),
            vmem_limit_bytes=VMEM_LIMIT),
        name="in_proj",
    )(x2, norm_w, wt_main, wt_dt, col_scale)


def _split_bf16(x, terms):
    parts = []
    r = x
    for _ in range(terms):
        p = r.astype(BF16)
        parts.append(p)
        r = r - p.astype(F32)
    return parts


def _ssd_kernel(xs_ref, bc_ref, dt_ref, z_ref, cwx_ref, cwb_ref, cbx_ref, cbb_ref,
                dtb_ref, alog_ref, dsk_ref, nw_ref, y_ref, extx, extb, shift, expand, state):
    L = CHUNK
    H0 = BF16_SUBLANES

    @pl.when(pl.program_id(1) == 0)
    def _():
        extx[:, 0:H0, :] = jnp.zeros((SSD_BPS, H0, D_SSM), BF16)
        extb[:, 0:H0, :] = jnp.zeros((SSD_BPS, H0, BC_W), BF16)
        state[...] = jnp.zeros(state.shape, F32)
        out_row = lax.broadcasted_iota(jnp.int32, shift.shape, 0)
        src_row = lax.broadcasted_iota(jnp.int32, shift.shape, 1)
        want = H0 + out_row % L - (CONV_WIDTH - 1) + out_row // L
        shift[...] = jnp.where(src_row == want, 1.0, 0.0).astype(BF16)
        er = lax.broadcasted_iota(jnp.int32, expand.shape, 0)
        ec = lax.broadcasted_iota(jnp.int32, expand.shape, 1)
        expand[...] = jnp.where(ec // SSM_HEAD_DIM == er, 1.0, 0.0).astype(BF16)

    for bb in range(SSD_BPS):
        _ssd_chunk(xs_ref.at[bb], bc_ref.at[bb], dt_ref.at[bb], z_ref.at[bb], cwx_ref, cwb_ref,
                   cbx_ref, cbb_ref, dtb_ref, alog_ref, dsk_ref, nw_ref, y_ref.at[bb],
                   extx.at[bb], extb.at[bb], shift, expand, state.at[bb])


def _ssd_chunk(xs_ref, bc_ref, dt_ref, z_ref, cwx_ref, cwb_ref, cbx_ref, cbb_ref,
               dtb_ref, alog_ref, dsk_ref, nw_ref, y_ref, extx, extb, shift, expand, state):
    L = CHUNK
    H0 = BF16_SUBLANES

    extx[H0:H0 + L, :] = xs_ref[...]
    extb[H0:H0 + L, :] = bc_ref[...]

    def conv(ext, w_ref, b_ref):
        taps = jnp.dot(shift[...], ext[...], preferred_element_type=F32)
        acc = b_ref[...] + w_ref[0:1, :] * taps[0:L]
        for w in range(1, CONV_WIDTH):
            acc = acc + w_ref[w:w + 1, :] * taps[w * L:(w + 1) * L]
        return _silu(acc)

    xs = conv(extx, cwx_ref, cbx_ref)
    bc = conv(extb, cwb_ref, cbb_ref)
    extx[0:H0, :] = extx[L:L + H0, :]
    extb[0:H0, :] = extb[L:L + H0, :]

    dtr = dt_ref[...] + dtb_ref[...]
    dt = jnp.maximum(dtr, 0.0) + jnp.log1p(jnp.exp(-jnp.abs(dtr)))
    a = dt * (-jnp.exp(alog_ref[...]))

    row = lax.broadcasted_iota(jnp.int32, (L, L), 0)
    col = lax.broadcasted_iota(jnp.int32, (L, L), 1)
    tril = row >= col
    tri = jnp.where(tril, 1.0, 0.0).astype(BF16)
    a3 = jnp.concatenate(_split_bf16(a, 3), axis=1)
    cs3 = jnp.dot(tri, a3, preferred_element_type=F32)
    cs = (cs3[:, 0:LANES] + cs3[:, LANES:2 * LANES]) + cs3[:, 2 * LANES:3 * LANES]
    cs_t = cs.T

    stack = jnp.concatenate(_split_bf16(dt, 2) + _split_bf16(cs, 2), axis=0)
    ex = jnp.dot(stack, expand[...], preferred_element_type=F32)
    dt_x = ex[0:L] + ex[L:2 * L]
    cs_x = ex[2 * L:3 * L] + ex[3 * L:4 * L]
    cs_last = cs_x[L - 1:L, :]

    xdt = xs * dt_x
    xdt_b = xdt.astype(BF16)
    xdec_b = (xdt * jnp.exp(cs_last - cs_x)).astype(BF16)
    dec_out = jnp.exp(cs_x)
    dec_chunk = jnp.exp(cs_last)

    lane = lax.broadcasted_iota(jnp.int32, (L, LANES), 1)
    low_half = lane < SSM_HEAD_DIM
    neg_inf = jnp.float32(-jnp.inf)

    y_parts = []
    for g in range(SSM_GROUPS):
        gsl = slice(g * GROUP_W, (g + 1) * GROUP_W)
        b_g = bc[:, g * SSM_STATE:(g + 1) * SSM_STATE].astype(BF16)
        c_g = bc[:, BC_W // 2 + g * SSM_STATE:BC_W // 2 + (g + 1) * SSM_STATE].astype(BF16)
        cb = lax.dot_general(c_g, b_g, (((1,), (1,)), ((), ())),
                             preferred_element_type=F32)
        s_in = state[g]
        y_off = jnp.dot(c_g, s_in.astype(BF16), preferred_element_type=F32) * dec_out[:, gsl]
        s_new = lax.dot_general(b_g, xdec_b[:, gsl], (((0,), (0,)), ((), ())),
                                preferred_element_type=F32)
        state[g] = s_in * dec_chunk[:, gsl] + s_new

        pair_out = []
        for pr in range(SSM_HPG // 2):
            ms = []
            for r in (2 * pr, 2 * pr + 1):
                h = g * SSM_HPG + r
                diff = cs[:, h:h + 1] - cs_t[h:h + 1, :]
                decay = jnp.exp(jnp.where(tril, diff, neg_inf))
                ms.append((cb * decay).astype(BF16))
            m2 = jnp.concatenate(ms, axis=1)
            c0 = g * GROUP_W + pr * LANES
            xp = xdt_b[:, c0:c0 + LANES]
            zero = jnp.zeros_like(xp)
            rhs = jnp.concatenate([jnp.where(low_half, xp, zero),
                                   jnp.where(low_half, zero, xp)], axis=0)
            pair_out.append(jnp.dot(m2, rhs, preferred_element_type=F32))
        y_parts.append(jnp.concatenate(pair_out, axis=1) + y_off)
    y = jnp.concatenate(y_parts, axis=1)

    y = y + dsk_ref[...] * xs
    y = y * _silu(z_ref[...].astype(F32))
    outs = []
    for g in range(SSM_GROUPS):
        gsl = slice(g * GROUP_W, (g + 1) * GROUP_W)
        yg = y[:, gsl]
        ms = jnp.mean(yg * yg, axis=-1, keepdims=True)
        outs.append((yg * lax.rsqrt(ms + NORM_EPS)) * nw_ref[:, gsl])
    y_ref[...] = jnp.concatenate(outs, axis=1).astype(y_ref.dtype)


def _ssd(proj3, dt3, cwx, cwb, cbx, cbb, dtb, alog, dsk, nw):
    b, s, _ = proj3.shape
    nc = s // CHUNK
    full = lambda shape: pl.BlockSpec(shape, lambda bi, ci: (0, 0))
    return pl.pallas_call(
        _ssd_kernel,
        grid=(b // SSD_BPS, nc),
        in_specs=[
            pl.BlockSpec((SSD_BPS, CHUNK, D_SSM), lambda bi, ci: (bi, ci, OFF_XS // D_SSM)),
            pl.BlockSpec((SSD_BPS, CHUNK, BC_W), lambda bi, ci: (bi, ci, OFF_BC // BC_W)),
            pl.BlockSpec((SSD_BPS, CHUNK, DT_PAD), lambda bi, ci: (bi, ci, 0)),
            pl.BlockSpec((SSD_BPS, CHUNK, D_SSM), lambda bi, ci: (bi, ci, OFF_ZS // D_SSM)),
            full((CONV_WIDTH, D_SSM)), full((CONV_WIDTH, BC_W)),
            full((1, D_SSM)), full((1, BC_W)),
            full((1, DT_PAD)), full((1, DT_PAD)),
            full((1, D_SSM)), full((1, D_SSM)),
        ],
        out_specs=pl.BlockSpec((SSD_BPS, CHUNK, D_SSM), lambda bi, ci: (bi, ci, 0)),
        out_shape=jax.ShapeDtypeStruct((b, s, D_SSM), BF16),
        scratch_shapes=[
            pltpu.VMEM((SSD_BPS, BF16_SUBLANES + CHUNK, D_SSM), BF16),
            pltpu.VMEM((SSD_BPS, BF16_SUBLANES + CHUNK, BC_W), BF16),
            pltpu.VMEM((CONV_WIDTH * CHUNK, BF16_SUBLANES + CHUNK), BF16),
            pltpu.VMEM((LANES, D_SSM), BF16),
            pltpu.VMEM((SSD_BPS, SSM_GROUPS, SSM_STATE, GROUP_W), F32),
        ],
        compiler_params=pltpu.CompilerParams(
            dimension_semantics=("parallel", "arbitrary"),
            vmem_limit_bytes=VMEM_LIMIT),
        name="ssd",
    )(proj3, proj3, dt3, proj3, cwx, cwb, cbx, cbb, dtb, alog, dsk, nw)


def _attn_kernel(lam_init, q_ref, k_ref, v_ref, z_ref, lamv_ref, sw_ref, o_ref,
                 vt_scr, bias_scr, s_scr, bm_scr, p_scr, acc_scr):
    i = pl.program_id(2)
    tq = tk = AT_T
    nkb = vt_scr.shape[1]
    heads = range(AT_HPS)
    streams = range(2 * AT_HPS)
    hcols = lambda hh: slice(hh * ATTN_VDIM, (hh + 1) * ATTN_VDIM)

    @pl.when(i == 0)
    def _():
        kpos = lax.broadcasted_iota(jnp.int32, (tk, tq), 0)
        qpos = lax.broadcasted_iota(jnp.int32, (tk, tq), 1)
        bias_scr[...] = jnp.where(kpos <= qpos, 0.0, -jnp.inf).astype(F32)
        ones = jnp.ones((AT_ACC_ROWS - ATTN_VDIM, tk), BF16)
        for hh in heads:
            for cblk in range(nkb):
                vb = v_ref[cblk * tk:(cblk + 1) * tk, hcols(hh)].astype(F32)
                vt_scr[hh, cblk, 0:ATTN_VDIM, :] = vb.T.astype(BF16)
                vt_scr[hh, cblk, ATTN_VDIM:AT_ACC_ROWS, :] = ones

    sub = lax.broadcasted_iota(jnp.int32, (ATTN_VDIM, tq), 0)
    q_t = []
    for hh in heads:
        qh_t = q_ref[:, hcols(hh)].astype(F32).T
        q_t.append(jnp.where(sub < ATTN_HEAD_DIM, qh_t, 0.0).astype(BF16))
        q_t.append(jnp.where(sub >= ATTN_HEAD_DIM, qh_t, 0.0).astype(BF16))

    acc_scr[...] = jnp.zeros(acc_scr.shape, F32)
    neg_inf = jnp.float32(-jnp.inf)

    def scores(j, n, diagonal=False):
        start = pl.multiple_of(j * tk, tk)
        k_b = k_ref[pl.ds(start, tk), hcols(n // 2)]
        s_scr[n] = jnp.dot(k_b, q_t[n], preferred_element_type=F32)
        if diagonal:
            s_scr[n] = s_scr[n] + bias_scr[...]
        bm_scr[n] = jnp.max(s_scr[n].reshape(tk // SUBLANES, SUBLANES, tq), axis=0)

    def softmax(m, n, bias=None):
        s, bm = s_scr[n], jnp.max(bm_scr[n], axis=0, keepdims=True)
        if bias is not None:
            s, bm = s + bias, bm + bias
        m_new = jnp.maximum(m, bm)
        p_scr[n] = jnp.exp2(s - m_new).astype(BF16)
        return m_new, jnp.exp2(m - m_new)

    def accumulate(vblk, n, alpha):
        vt_b = vt_scr[n // 2, vblk]
        acc_scr[n] = alpha * acc_scr[n] + jnp.dot(
            vt_b, p_scr[n], preferred_element_type=F32)

    prev_block = lambda j: jnp.where(j <= 0, i, j - 1)

    def step(j, carry):
        ms, alphas = carry
        new = []
        for n in streams:
            accumulate(prev_block(j), n, alphas[n])
            new.append(softmax(ms[n], n))
            scores(j + 1, n)
        return tuple(x[0] for x in new), tuple(x[1] for x in new)

    for n in streams:
        scores(i, n, diagonal=True)
    first = []
    for n in streams:
        first.append(softmax(jnp.full((1, tq), neg_inf, F32), n))
        scores(0, n)
    carry = (tuple(x[0] for x in first), tuple(x[1] for x in first))
    carry = lax.fori_loop(0, jnp.maximum(i - 1, 0), step, carry)
    ms, alphas = carry
    ghost = jnp.where(i == 0, neg_inf, jnp.float32(0.0))
    last = []
    for n in streams:
        accumulate(prev_block(i - 1), n, alphas[n])
        last.append(softmax(ms[n], n, bias=ghost)[1])
    for n in streams:
        accumulate(jnp.maximum(i - 1, 0), n, last[n])

    lv = lamv_ref[...]
    lam = (jnp.exp(jnp.sum(lv[0:1] * lv[1:2], axis=-1, keepdims=True))
           - jnp.exp(jnp.sum(lv[2:3] * lv[3:4], axis=-1, keepdims=True))
           + lam_init)
    out_w = sw_ref[...] * (1.0 - lam_init)
    for hh in heads:
        n1, n2 = 2 * hh, 2 * hh + 1
        r1 = 1.0 / acc_scr[n1, ATTN_VDIM:ATTN_VDIM + 1, :]
        r2 = lam * (1.0 / acc_scr[n2, ATTN_VDIM:ATTN_VDIM + 1, :])
        o_t = acc_scr[n1, 0:ATTN_VDIM, :] * r1 - acc_scr[n2, 0:ATTN_VDIM, :] * r2
        o = o_t.T
        ms = jnp.mean(o * o, axis=-1, keepdims=True)
        on = (o * lax.rsqrt(ms + NORM_EPS)) * out_w
        gate = _silu(z_ref[:, hcols(hh)].astype(F32))
        o_ref[:, hcols(hh)] = (on * gate).astype(o_ref.dtype)


def _diff_attn(proj3, lamv, subln_w, lam_init):
    b, s, _ = proj3.shape
    nq = s // AT_T
    gw = AT_HPS * ATTN_VDIM
    gb = lambda off: off // gw
    nstream = 2 * AT_HPS
    return pl.pallas_call(
        functools.partial(_attn_kernel, lam_init),
        grid=(b, ATTN_HEADS // AT_HPS, nq),
        in_specs=[
            pl.BlockSpec((None, AT_T, gw), lambda bi, g, i: (bi, i, gb(OFF_Q) + g)),
            pl.BlockSpec((None, s, gw), lambda bi, g, i: (bi, 0, gb(OFF_K) + g)),
            pl.BlockSpec((None, s, gw), lambda bi, g, i: (bi, 0, gb(OFF_V) + g)),
            pl.BlockSpec((None, AT_T, gw), lambda bi, g, i: (bi, i, gb(OFF_ZA) + g)),
            pl.BlockSpec((4, ATTN_HEAD_DIM), lambda bi, g, i: (0, 0)),
            pl.BlockSpec((1, ATTN_VDIM), lambda bi, g, i: (0, 0)),
        ],
        out_specs=pl.BlockSpec((None, AT_T, gw), lambda bi, g, i: (bi, i, g)),
        out_shape=jax.ShapeDtypeStruct((b, s, D_ATTN), BF16),
        scratch_shapes=[
            pltpu.VMEM((AT_HPS, s // AT_T, AT_ACC_ROWS, AT_T), BF16),
            pltpu.VMEM((AT_T, AT_T), F32),
            pltpu.VMEM((nstream, AT_T, AT_T), F32),
            pltpu.VMEM((nstream, SUBLANES, AT_T), F32),
            pltpu.VMEM((nstream, AT_T, AT_T), BF16),
            pltpu.VMEM((nstream, AT_ACC_ROWS, AT_T), F32),
        ],
        compiler_params=pltpu.CompilerParams(
            dimension_semantics=("parallel", "parallel", "arbitrary"),
            vmem_limit_bytes=VMEM_LIMIT),
        name="diffattn",
    )(proj3, proj3, proj3, proj3, lamv, subln_w)


def _out_proj_kernel(ys_ref, ya_ref, w_ref, x_ref, nw_ref, o_ref):
    ys = ys_ref[...]
    ya = ya_ref[...]
    ssq = jnp.zeros((OUT_TM, 1), F32)
    for jj in range(D_MODEL // OUT_TN):
        sl = slice(jj * OUT_TN, (jj + 1) * OUT_TN)
        acc = (jnp.dot(ys, w_ref[0:D_SSM, sl], preferred_element_type=F32)
               + jnp.dot(ya, w_ref[D_SSM:D_SSM + D_ATTN, sl], preferred_element_type=F32))
        ssq = ssq + jnp.sum(acc * acc, axis=-1, keepdims=True)
        o_ref[:, sl] = acc
    inv = lax.rsqrt(ssq * (1.0 / D_MODEL) + NORM_EPS)
    for r in range(OUT_TM // OUT_RC):
        rows = slice(r * OUT_RC, (r + 1) * OUT_RC)
        inv_r = inv[rows]
        for jj in range(D_MODEL // OUT_TN):
            sl = slice(jj * OUT_TN, (jj + 1) * OUT_TN)
            o_ref[rows, sl] = x_ref[rows, sl] + (o_ref[rows, sl] * inv_r) * nw_ref[:, sl]


def _out_proj(ys, ya, w_o, x2, norm_w):
    m = x2.shape[0]
    return pl.pallas_call(
        _out_proj_kernel,
        grid=(m // OUT_TM,),
        in_specs=[
            pl.BlockSpec((OUT_TM, D_SSM), lambda i: (i, 0)),
            pl.BlockSpec((OUT_TM, D_ATTN), lambda i: (i, 0)),
            pl.BlockSpec((D_SSM + D_ATTN, D_MODEL), lambda i: (0, 0),
                         pipeline_mode=pl.Buffered(1)),
            pl.BlockSpec((OUT_TM, D_MODEL), lambda i: (i, 0)),
            pl.BlockSpec((1, D_MODEL), lambda i: (0, 0)),
        ],
        out_specs=pl.BlockSpec((OUT_TM, D_MODEL), lambda i: (i, 0)),
        out_shape=jax.ShapeDtypeStruct((m, D_MODEL), F32),
        compiler_params=pltpu.CompilerParams(
            dimension_semantics=("parallel",),
            vmem_limit_bytes=OUT_VMEM_LIMIT),
        name="out_proj",
    )(ys, ya, w_o, x2, norm_w)


def _layer(x, layer, pre_norm_w, post_norm_w, w_in, conv_w, conv_b, dt_bias, a_log, d_skip,
           ssm_norm_w, lambda_q1, lambda_k1, lambda_q2, lambda_k2, attn_subln_w, w_out):
    b, s, d = x.shape
    x2 = x.reshape(b * s, d)

    wt_main = w_in.T.astype(BF16)
    wt_dt = jnp.pad(wt_main[W_IN_DT:W_IN_QKV], ((0, DT_PAD - SSM_HEADS), (0, 0)))

    q_scale = ATTN_HEAD_DIM ** -0.5 * math.log2(math.e)
    col_scale = jnp.ones((1, PROJ_W), F32).at[:, OFF_Q:OFF_Q + D_ATTN].set(q_scale)
    proj, dt_raw = _in_proj(x2, pre_norm_w.reshape(1, d), wt_main, wt_dt, col_scale)
    proj3 = proj.reshape(b, s, PROJ_W)
    dt3 = dt_raw.reshape(b, s, DT_PAD)

    pad_h = lambda v: jnp.pad(v.reshape(1, SSM_HEADS), ((0, 0), (0, DT_PAD - SSM_HEADS)))
    y_ssm = _ssd(
        proj3, dt3,
        conv_w[:, :D_SSM], conv_w[:, D_SSM:],
        conv_b[:D_SSM].reshape(1, D_SSM), conv_b[D_SSM:].reshape(1, BC_W),
        pad_h(dt_bias), pad_h(a_log),
        jnp.repeat(d_skip, SSM_HEAD_DIM).reshape(1, D_SSM),
        ssm_norm_w.reshape(1, D_SSM))

    lam_init = 0.8 - 0.6 * math.exp(-0.3 * layer)
    lamv = jnp.stack([lambda_q1, lambda_k1, lambda_q2, lambda_k2], axis=0)
    y_attn = _diff_attn(proj3, lamv, attn_subln_w.reshape(1, ATTN_VDIM), lam_init)

    out = _out_proj(y_ssm.reshape(b * s, D_SSM), y_attn.reshape(b * s, D_ATTN),
                    w_out.astype(BF16), x2, post_norm_w.reshape(1, d))
    return out.reshape(b, s, d)


def kernel(x, pre_norm_w, post_norm_w, w_in, conv_w, conv_b, dt_bias, a_log, d_skip, ssm_norm_w, lambda_q1, lambda_k1, lambda_q2, lambda_k2, attn_subln_w, w_out):
    for layer in range(w_in.shape[0]):
        x = _layer(x, layer, pre_norm_w[layer], post_norm_w[layer], w_in[layer],
                   conv_w[layer], conv_b[layer], dt_bias[layer], a_log[layer], d_skip[layer],
                   ssm_norm_w[layer], lambda_q1[layer], lambda_k1[layer], lambda_q2[layer],
                   lambda_k2[layer], attn_subln_w[layer], w_out[layer])
    return x
```

```python
import functools
import math

import jax
import jax.numpy as jnp
from jax import lax
from jax.experimental import pallas as pl
from jax.experimental.pallas import tpu as pltpu

F32 = jnp.float32
BF16 = jnp.bfloat16

D_MODEL = 4096
D_SSM = 2048
D_ATTN = 2048
SSM_HEAD_DIM = 64
SSM_HEADS = 32
SSM_GROUPS = 4
SSM_HPG = 8
SSM_STATE = 128
CONV_WIDTH = 4
CHUNK = 128
ATTN_HEAD_DIM = 64
ATTN_VDIM = 128
ATTN_HEADS = 16
NORM_EPS = 1e-6
GROUP_W = D_SSM // SSM_GROUPS
BC_W = 2 * SSM_GROUPS * SSM_STATE

LANES = 128
SUBLANES = 8
VMEM_LIMIT = 56 * 1024 * 1024

OFF_ZS = 0
OFF_ZA = 2048
OFF_XS = 4096
OFF_BC = 6144
OFF_Q = 7168
OFF_K = 9216
OFF_V = 11264
PROJ_W = 13312
W_IN_DT = OFF_Q
W_IN_QKV = W_IN_DT + SSM_HEADS
DT_PAD = LANES

IN_TM, IN_TN = 1024, 1024
IN_XR = 512
IN_NP = IN_TM // IN_XR
AT_T = 512
SSD_BPS = 2
AT_HPS = 4
BF16_SUBLANES = 16
AT_ACC_ROWS = ATTN_VDIM + BF16_SUBLANES
OUT_TM, OUT_TN, OUT_RC = 256, 512, 64
OUT_VMEM_LIMIT = 60 * 1024 * 1024


def _silu(x):
    hx = 0.5 * x
    return hx + hx * jnp.tanh(hx)


def _dot_nt(a, b_t):
    return lax.dot_general(a, b_t, (((1,), (1,)), ((), ())), preferred_element_type=F32)


def _in_proj_kernel(x_ref, nw_ref, wt_ref, wdt_ref, cs_ref, o_ref, dt_ref, h_scr):
    s = pl.program_id(1)

    for p in range(IN_NP):
        @pl.when(s == p)
        def _(p=p):
            x = x_ref[...]
            ms = jnp.mean(x * x, axis=-1, keepdims=True)
            h = (x * lax.rsqrt(ms + NORM_EPS)) * nw_ref[...]
            hb = h.astype(BF16)
            h_scr[p * IN_XR:(p + 1) * IN_XR, :] = hb
            dt_ref[...] = _dot_nt(hb, wdt_ref[...])

    @pl.when(s >= IN_NP)
    def _():
        o_ref[...] = (_dot_nt(h_scr[...], wt_ref[...]) * cs_ref[...]).astype(o_ref.dtype)


def _in_proj(x2, norm_w, wt_main, wt_dt, col_scale):
    m = x2.shape[0]
    xrow = lambda i, s: i * IN_NP + jnp.minimum(s, IN_NP - 1)
    col = lambda s: jnp.maximum(s - IN_NP, 0)
    return pl.pallas_call(
        _in_proj_kernel,
        grid=(m // IN_TM, IN_NP + PROJ_W // IN_TN),
        in_specs=[
            pl.BlockSpec((IN_XR, D_MODEL), lambda i, s: (xrow(i, s), 0)),
            pl.BlockSpec((1, D_MODEL), lambda i, s: (0, 0)),
            pl.BlockSpec((pl.Element(IN_TN), pl.Element(D_MODEL)),
                         lambda i, s: (BF16_SUBLANES * (col(s) * (IN_TN // BF16_SUBLANES) + jnp.where(
                             col(s) * IN_TN >= W_IN_DT, SSM_HEADS // BF16_SUBLANES, 0)), 0)),
            pl.BlockSpec((DT_PAD, D_MODEL), lambda i, s: (0, 0)),
            pl.BlockSpec((1, IN_TN), lambda i, s: (0, col(s))),
        ],
        out_specs=[
            pl.BlockSpec((IN_TM, IN_TN), lambda i, s: (i, col(s))),
            pl.BlockSpec((IN_XR, DT_PAD), lambda i, s: (xrow(i, s), 0)),
        ],
        out_shape=[
            jax.ShapeDtypeStruct((m, PROJ_W), BF16),
            jax.ShapeDtypeStruct((m, DT_PAD), F32),
        ],
        scratch_shapes=[pltpu.VMEM((IN_TM, D_MODEL), BF16)],
        compiler_params=pltpu.CompilerParams(
            dimension_semantics=("parallel", "arbitrary"),
            vmem_limit_bytes=VMEM_LIMIT),
        name="in_proj",
    )(x2, norm_w, wt_main, wt_dt, col_scale)


def _split_bf16(x, terms):
    parts = []
    r = x
    for _ in range(terms):
        p = r.astype(BF16)
        parts.append(p)
        r = r - p.astype(F32)
    return parts


def _ssd_kernel(xs_ref, bc_ref, dt_ref, z_ref, cwx_ref, cwb_ref, cbx_ref, cbb_ref,
                dtb_ref, alog_ref, dsk_ref, nw_ref, y_ref, extx, extb, shift, expand, state):
    L = CHUNK
    H0 = BF16_SUBLANES

    @pl.when(pl.program_id(1) == 0)
    def _():
        extx[:, 0:H0, :] = jnp.zeros((SSD_BPS, H0, D_SSM), BF16)
        extb[:, 0:H0, :] = jnp.zeros((SSD_BPS, H0, BC_W), BF16)
        state[...] = jnp.zeros(state.shape, F32)
        out_row = lax.broadcasted_iota(jnp.int32, shift.shape, 0)
        src_row = lax.broadcasted_iota(jnp.int32, shift.shape, 1)
        want = H0 + out_row % L - (CONV_WIDTH - 1) + out_row // L
        shift[...] = jnp.where(src_row == want, 1.0, 0.0).astype(BF16)
        er = lax.broadcasted_iota(jnp.int32, expand.shape, 0)
        ec = lax.broadcasted_iota(jnp.int32, expand.shape, 1)
        expand[...] = jnp.where(ec // SSM_HEAD_DIM == er, 1.0, 0.0).astype(BF16)

    for bb in range(SSD_BPS):
        _ssd_chunk(xs_ref.at[bb], bc_ref.at[bb], dt_ref.at[bb], z_ref.at[bb], cwx_ref, cwb_ref,
                   cbx_ref, cbb_ref, dtb_ref, alog_ref, dsk_ref, nw_ref, y_ref.at[bb],
                   extx.at[bb], extb.at[bb], shift, expand, state.at[bb])


def _ssd_chunk(xs_ref, bc_ref, dt_ref, z_ref, cwx_ref, cwb_ref, cbx_ref, cbb_ref,
               dtb_ref, alog_ref, dsk_ref, nw_ref, y_ref, extx, extb, shift, expand, state):
    L = CHUNK
    H0 = BF16_SUBLANES

    extx[H0:H0 + L, :] = xs_ref[...]
    extb[H0:H0 + L, :] = bc_ref[...]

    def conv(ext, w_ref, b_ref):
        taps = jnp.dot(shift[...], ext[...], preferred_element_type=F32)
        acc = b_ref[...] + w_ref[0:1, :] * taps[0:L]
        for w in range(1, CONV_WIDTH):
            acc = acc + w_ref[w:w + 1, :] * taps[w * L:(w + 1) * L]
        return _silu(acc)

    xs = conv(extx, cwx_ref, cbx_ref)
    bc = conv(extb, cwb_ref, cbb_ref)
    extx[0:H0, :] = extx[L:L + H0, :]
    extb[0:H0, :] = extb[L:L + H0, :]

    dtr = dt_ref[...] + dtb_ref[...]
    dt = jnp.maximum(dtr, 0.0) + jnp.log1p(jnp.exp(-jnp.abs(dtr)))
    a = dt * (-jnp.exp(alog_ref[...]))

    row = lax.broadcasted_iota(jnp.int32, (L, L), 0)
    col = lax.broadcasted_iota(jnp.int32, (L, L), 1)
    tril = row >= col
    tri = jnp.where(tril, 1.0, 0.0).astype(BF16)
    a3 = jnp.concatenate(_split_bf16(a, 3), axis=1)
    cs3 = jnp.dot(tri, a3, preferred_element_type=F32)
    cs = (cs3[:, 0:LANES] + cs3[:, LANES:2 * LANES]) + cs3[:, 2 * LANES:3 * LANES]
    cs_t = cs.T

    stack = jnp.concatenate(_split_bf16(dt, 2) + _split_bf16(cs, 2), axis=0)
    ex = jnp.dot(stack, expand[...], preferred_element_type=F32)
    dt_x = ex[0:L] + ex[L:2 * L]
    cs_x = ex[2 * L:3 * L] + ex[3 * L:4 * L]
    cs_last = cs_x[L - 1:L, :]

    xdt = xs * dt_x
    xdt_b = xdt.astype(BF16)
    xdec_b = (xdt * jnp.exp(cs_last - cs_x)).astype(BF16)
    dec_out = jnp.exp(cs_x)
    dec_chunk = jnp.exp(cs_last)

    lane = lax.broadcasted_iota(jnp.int32, (L, LANES), 1)
    low_half = lane < SSM_HEAD_DIM
    neg_inf = jnp.float32(-jnp.inf)

    y_parts = []
    for g in range(SSM_GROUPS):
        gsl = slice(g * GROUP_W, (g + 1) * GROUP_W)
        b_g = bc[:, g * SSM_STATE:(g + 1) * SSM_STATE].astype(BF16)
        c_g = bc[:, BC_W // 2 + g * SSM_STATE:BC_W // 2 + (g + 1) * SSM_STATE].astype(BF16)
        cb = lax.dot_general(c_g, b_g, (((1,), (1,)), ((), ())),
                             preferred_element_type=F32)
        s_in = state[g]
        y_off = jnp.dot(c_g, s_in.astype(BF16), preferred_element_type=F32) * dec_out[:, gsl]
        s_new = lax.dot_general(b_g, xdec_b[:, gsl], (((0,), (0,)), ((), ())),
                                preferred_element_type=F32)
        state[g] = s_in * dec_chunk[:, gsl] + s_new

        pair_out = []
        for pr in range(SSM_HPG // 2):
            ms = []
            for r in (2 * pr, 2 * pr + 1):
                h = g * SSM_HPG + r
                diff = cs[:, h:h + 1] - cs_t[h:h + 1, :]
                decay = jnp.exp(jnp.where(tril, diff, neg_inf))
                ms.append((cb * decay).astype(BF16))
            m2 = jnp.concatenate(ms, axis=1)
            c0 = g * GROUP_W + pr * LANES
            xp = xdt_b[:, c0:c0 + LANES]
            zero = jnp.zeros_like(xp)
            rhs = jnp.concatenate([jnp.where(low_half, xp, zero),
                                   jnp.where(low_half, zero, xp)], axis=0)
            pair_out.append(jnp.dot(m2, rhs, preferred_element_type=F32))
        y_parts.append(jnp.concatenate(pair_out, axis=1) + y_off)
    y = jnp.concatenate(y_parts, axis=1)

    y = y + dsk_ref[...] * xs
    y = y * _silu(z_ref[...].astype(F32))
    outs = []
    for g in range(SSM_GROUPS):
        gsl = slice(g * GROUP_W, (g + 1) * GROUP_W)
        yg = y[:, gsl]
        ms = jnp.mean(yg * yg, axis=-1, keepdims=True)
        outs.append((yg * lax.rsqrt(ms + NORM_EPS)) * nw_ref[:, gsl])
    y_ref[...] = jnp.concatenate(outs, axis=1).astype(y_ref.dtype)


def _ssd(proj3, dt3, cwx, cwb, cbx, cbb, dtb, alog, dsk, nw):
    b, s, _ = proj3.shape
    nc = s // CHUNK
    full = lambda shape: pl.BlockSpec(shape, lambda bi, ci: (0, 0))
    return pl.pallas_call(
        _ssd_kernel,
        grid=(b // SSD_BPS, nc),
        in_specs=[
            pl.BlockSpec((SSD_BPS, CHUNK, D_SSM), lambda bi, ci: (bi, ci, OFF_XS // D_SSM)),
            pl.BlockSpec((SSD_BPS, CHUNK, BC_W), lambda bi, ci: (bi, ci, OFF_BC // BC_W)),
            pl.BlockSpec((SSD_BPS, CHUNK, DT_PAD), lambda bi, ci: (bi, ci, 0)),
            pl.BlockSpec((SSD_BPS, CHUNK, D_SSM), lambda bi, ci: (bi, ci, OFF_ZS // D_SSM)),
            full((CONV_WIDTH, D_SSM)), full((CONV_WIDTH, BC_W)),
            full((1, D_SSM)), full((1, BC_W)),
            full((1, DT_PAD)), full((1, DT_PAD)),
            full((1, D_SSM)), full((1, D_SSM)),
        ],
        out_specs=pl.BlockSpec((SSD_BPS, CHUNK, D_SSM), lambda bi, ci: (bi, ci, 0)),
        out_shape=jax.ShapeDtypeStruct((b, s, D_SSM), BF16),
        scratch_shapes=[
            pltpu.VMEM((SSD_BPS, BF16_SUBLANES + CHUNK, D_SSM), BF16),
            pltpu.VMEM((SSD_BPS, BF16_SUBLANES + CHUNK, BC_W), BF16),
            pltpu.VMEM((CONV_WIDTH * CHUNK, BF16_SUBLANES + CHUNK), BF16),
            pltpu.VMEM((LANES, D_SSM), BF16),
            pltpu.VMEM((SSD_BPS, SSM_GROUPS, SSM_STATE, GROUP_W), F32),
        ],
        compiler_params=pltpu.CompilerParams(
            dimension_semantics=("parallel", "arbitrary"),
            vmem_limit_bytes=VMEM_LIMIT),
        name="ssd",
    )(proj3, proj3, dt3, proj3, cwx, cwb, cbx, cbb, dtb, alog, dsk, nw)


def _attn_kernel(lam_init, q_ref, k_ref, v_ref, z_ref, lamv_ref, sw_ref, o_ref,
                 vt_scr, bias_scr, s_scr, bm_scr, p_scr, acc_scr):
    i = pl.program_id(2)
    tq = tk = AT_T
    nkb = vt_scr.shape[1]
    heads = range(AT_HPS)
    streams = range(2 * AT_HPS)
    hcols = lambda hh: slice(hh * ATTN_VDIM, (hh + 1) * ATTN_VDIM)

    @pl.when(i == 0)
    def _():
        kpos = lax.broadcasted_iota(jnp.int32, (tk, tq), 0)
        qpos = lax.broadcasted_iota(jnp.int32, (tk, tq), 1)
        bias_scr[...] = jnp.where(kpos <= qpos, 0.0, -jnp.inf).astype(F32)
        ones = jnp.ones((AT_ACC_ROWS - ATTN_VDIM, tk), BF16)
        for hh in heads:
            for cblk in range(nkb):
                vb = v_ref[cblk * tk:(cblk + 1) * tk, hcols(hh)].astype(F32)
                vt_scr[hh, cblk, 0:ATTN_VDIM, :] = vb.T.astype(BF16)
                vt_scr[hh, cblk, ATTN_VDIM:AT_ACC_ROWS, :] = ones

    sub = lax.broadcasted_iota(jnp.int32, (ATTN_VDIM, tq), 0)
    q_t = []
    for hh in heads:
        qh_t = q_ref[:, hcols(hh)].astype(F32).T
        q_t.append(jnp.where(sub < ATTN_HEAD_DIM, qh_t, 0.0).astype(BF16))
        q_t.append(jnp.where(sub >= ATTN_HEAD_DIM, qh_t, 0.0).astype(BF16))

    acc_scr[...] = jnp.zeros(acc_scr.shape, F32)
    neg_inf = jnp.float32(-jnp.inf)

    def scores(j, n, diagonal=False):
        start = pl.multiple_of(j * tk, tk)
        k_b = k_ref[pl.ds(start, tk), hcols(n // 2)]
        s_scr[n] = jnp.dot(k_b, q_t[n], preferred_element_type=F32)
        if diagonal:
            s_scr[n] = s_scr[n] + bias_scr[...]
        bm_scr[n] = jnp.max(s_scr[n].reshape(tk // SUBLANES, SUBLANES, tq), axis=0)

    def softmax(m, n, ghost=None):
        s, bm = s_scr[n], jnp.max(bm_scr[n], axis=0, keepdims=True)
        if ghost is None:
            m_new = m_sub = jnp.maximum(m, bm)
        else:
            m_new = jnp.maximum(m, bm + ghost)
            m_sub = m_new - ghost
        p_scr[n] = jnp.exp2(s - m_sub).astype(BF16)
        return m_new, jnp.exp2(m - m_new)

    def accumulate(vblk, n, alpha):
        vt_b = vt_scr[n // 2, vblk]
        acc_scr[n] = alpha * acc_scr[n] + jnp.dot(
            vt_b, p_scr[n], preferred_element_type=F32)

    prev_block = lambda j: jnp.where(j <= 0, i, j - 1)

    def step(j, carry):
        ms, alphas = carry
        new = []
        for n in streams:
            accumulate(prev_block(j), n, alphas[n])
            new.append(softmax(ms[n], n))
            scores(j + 1, n)
        return tuple(x[0] for x in new), tuple(x[1] for x in new)

    for n in streams:
        scores(i, n, diagonal=True)
    first = []
    for n in streams:
        first.append(softmax(jnp.full((1, tq), neg_inf, F32), n))
        scores(0, n)
    carry = (tuple(x[0] for x in first), tuple(x[1] for x in first))
    carry = lax.fori_loop(0, jnp.maximum(i - 1, 0), step, carry)
    ms, alphas = carry
    ghost = jnp.where(i == 0, neg_inf, jnp.float32(0.0))
    last = []
    for n in streams:
        accumulate(prev_block(i - 1), n, alphas[n])
        last.append(softmax(ms[n], n, ghost=ghost)[1])
    for n in streams:
        accumulate(jnp.maximum(i - 1, 0), n, last[n])

    lv = lamv_ref[...]
    lam = (jnp.exp(jnp.sum(lv[0:1] * lv[1:2], axis=-1, keepdims=True))
           - jnp.exp(jnp.sum(lv[2:3] * lv[3:4], axis=-1, keepdims=True))
           + lam_init)
    out_w = sw_ref[...] * (1.0 - lam_init)
    for hh in heads:
        n1, n2 = 2 * hh, 2 * hh + 1
        r1 = 1.0 / acc_scr[n1, ATTN_VDIM:ATTN_VDIM + 1, :]
        r2 = lam * (1.0 / acc_scr[n2, ATTN_VDIM:ATTN_VDIM + 1, :])
        o_t = acc_scr[n1, 0:ATTN_VDIM, :] * r1 - acc_scr[n2, 0:ATTN_VDIM, :] * r2
        o = o_t.T
        ms = jnp.mean(o * o, axis=-1, keepdims=True)
        on = (o * lax.rsqrt(ms + NORM_EPS)) * out_w
        gate = _silu(z_ref[:, hcols(hh)].astype(F32))
        o_ref[:, hcols(hh)] = (on * gate).astype(o_ref.dtype)


def _diff_attn(proj3, lamv, subln_w, lam_init):
    b, s, _ = proj3.shape
    nq = s // AT_T
    gw = AT_HPS * ATTN_VDIM
    gb = lambda off: off // gw
    nstream = 2 * AT_HPS
    return pl.pallas_call(
        functools.partial(_attn_kernel, lam_init),
        grid=(b, ATTN_HEADS // AT_HPS, nq),
        in_specs=[
            pl.BlockSpec((None, AT_T, gw), lambda bi, g, i: (bi, i, gb(OFF_Q) + g)),
            pl.BlockSpec((None, s, gw), lambda bi, g, i: (bi, 0, gb(OFF_K) + g)),
            pl.BlockSpec((None, s, gw), lambda bi, g, i: (bi, 0, gb(OFF_V) + g)),
            pl.BlockSpec((None, AT_T, gw), lambda bi, g, i: (bi, i, gb(OFF_ZA) + g)),
            pl.BlockSpec((4, ATTN_HEAD_DIM), lambda bi, g, i: (0, 0)),
            pl.BlockSpec((1, ATTN_VDIM), lambda bi, g, i: (0, 0)),
        ],
        out_specs=pl.BlockSpec((None, AT_T, gw), lambda bi, g, i: (bi, i, g)),
        out_shape=jax.ShapeDtypeStruct((b, s, D_ATTN), BF16),
        scratch_shapes=[
            pltpu.VMEM((AT_HPS, s // AT_T, AT_ACC_ROWS, AT_T), BF16),
            pltpu.VMEM((AT_T, AT_T), F32),
            pltpu.VMEM((nstream, AT_T, AT_T), F32),
            pltpu.VMEM((nstream, SUBLANES, AT_T), F32),
            pltpu.VMEM((nstream, AT_T, AT_T), BF16),
            pltpu.VMEM((nstream, AT_ACC_ROWS, AT_T), F32),
        ],
        compiler_params=pltpu.CompilerParams(
            dimension_semantics=("parallel", "parallel", "arbitrary"),
            vmem_limit_bytes=VMEM_LIMIT),
        name="diffattn",
    )(proj3, proj3, proj3, proj3, lamv, subln_w)


def _out_proj_kernel(ys_ref, ya_ref, w_ref, x_ref, nw_ref, o_ref):
    ys = ys_ref[...]
    ya = ya_ref[...]
    ssq = jnp.zeros((OUT_TM, 1), F32)
    for jj in range(D_MODEL // OUT_TN):
        sl = slice(jj * OUT_TN, (jj + 1) * OUT_TN)
        acc = (jnp.dot(ys, w_ref[0:D_SSM, sl], preferred_element_type=F32)
               + jnp.dot(ya, w_ref[D_SSM:D_SSM + D_ATTN, sl], preferred_element_type=F32))
        ssq = ssq + jnp.sum(acc * acc, axis=-1, keepdims=True)
        o_ref[:, sl] = acc
    inv = lax.rsqrt(ssq * (1.0 / D_MODEL) + NORM_EPS)
    for r in range(OUT_TM // OUT_RC):
        rows = slice(r * OUT_RC, (r + 1) * OUT_RC)
        inv_r = inv[rows]
        for jj in range(D_MODEL // OUT_TN):
            sl = slice(jj * OUT_TN, (jj + 1) * OUT_TN)
            o_ref[rows, sl] = x_ref[rows, sl] + (o_ref[rows, sl] * inv_r) * nw_ref[:, sl]


def _out_proj(ys, ya, w_o, x2, norm_w):
    m = x2.shape[0]
    return pl.pallas_call(
        _out_proj_kernel,
        grid=(m // OUT_TM,),
        in_specs=[
            pl.BlockSpec((OUT_TM, D_SSM), lambda i: (i, 0)),
            pl.BlockSpec((OUT_TM, D_ATTN), lambda i: (i, 0)),
            pl.BlockSpec((D_SSM + D_ATTN, D_MODEL), lambda i: (0, 0),
                         pipeline_mode=pl.Buffered(1)),
            pl.BlockSpec((OUT_TM, D_MODEL), lambda i: (i, 0)),
            pl.BlockSpec((1, D_MODEL), lambda i: (0, 0)),
        ],
        out_specs=pl.BlockSpec((OUT_TM, D_MODEL), lambda i: (i, 0)),
        out_shape=jax.ShapeDtypeStruct((m, D_MODEL), F32),
        compiler_params=pltpu.CompilerParams(
            dimension_semantics=("parallel",),
            vmem_limit_bytes=OUT_VMEM_LIMIT),
        name="out_proj",
    )(ys, ya, w_o, x2, norm_w)


def _layer(x, layer, pre_norm_w, post_norm_w, w_in, conv_w, conv_b, dt_bias, a_log, d_skip,
           ssm_norm_w, lambda_q1, lambda_k1, lambda_q2, lambda_k2, attn_subln_w, w_out):
    b, s, d = x.shape
    x2 = x.reshape(b * s, d)

    wt_main = w_in.T.astype(BF16)
    wt_dt = jnp.pad(wt_main[W_IN_DT:W_IN_QKV], ((0, DT_PAD - SSM_HEADS), (0, 0)))

    q_scale = ATTN_HEAD_DIM ** -0.5 * math.log2(math.e)
    col_scale = jnp.ones((1, PROJ_W), F32).at[:, OFF_Q:OFF_Q + D_ATTN].set(q_scale)
    proj, dt_raw = _in_proj(x2, pre_norm_w.reshape(1, d), wt_main, wt_dt, col_scale)
    proj3 = proj.reshape(b, s, PROJ_W)
    dt3 = dt_raw.reshape(b, s, DT_PAD)

    pad_h = lambda v: jnp.pad(v.reshape(1, SSM_HEADS), ((0, 0), (0, DT_PAD - SSM_HEADS)))
    y_ssm = _ssd(
        proj3, dt3,
        conv_w[:, :D_SSM], conv_w[:, D_SSM:],
        conv_b[:D_SSM].reshape(1, D_SSM), conv_b[D_SSM:].reshape(1, BC_W),
        pad_h(dt_bias), pad_h(a_log),
        jnp.repeat(d_skip, SSM_HEAD_DIM).reshape(1, D_SSM),
        ssm_norm_w.reshape(1, D_SSM))

    lam_init = 0.8 - 0.6 * math.exp(-0.3 * layer)
    lamv = jnp.stack([lambda_q1, lambda_k1, lambda_q2, lambda_k2], axis=0)
    y_attn = _diff_attn(proj3, lamv, attn_subln_w.reshape(1, ATTN_VDIM), lam_init)

    out = _out_proj(y_ssm.reshape(b * s, D_SSM), y_attn.reshape(b * s, D_ATTN),
                    w_out.astype(BF16), x2, post_norm_w.reshape(1, d))
    return out.reshape(b, s, d)


def kernel(x, pre_norm_w, post_norm_w, w_in, conv_w, conv_b, dt_bias, a_log, d_skip, ssm_norm_w, lambda_q1, lambda_k1, lambda_q2, lambda_k2, attn_subln_w, w_out):
    for layer in range(w_in.shape[0]):
        x = _layer(x, layer, pre_norm_w[layer], post_norm_w[layer], w_in[layer],
                   conv_w[layer], conv_b[layer], dt_bias[layer], a_log[layer], d_skip[layer],
                   ssm_norm_w[layer], lambda_q1[layer], lambda_k1[layer], lambda_q2[layer],
                   lambda_k2[layer], attn_subln_w[layer], w_out[layer])
    return x
```
